```python
import math
import jax, jax.numpy as jnp
from jax import lax
import numpy as np

D_MODEL = 1024
BATCH = 16
SEQ = 2048
DEPTH = 4
DEC_BATCH = 128
DEC_SEQ = 4
PAST_LEN = 8192
PAGE_SIZE = 128

MLA_HEADS = 16
MLA_NOPE = 64
MLA_ROPE = 32
MLA_VDIM = 64
MLA_Q_LORA = 256
MLA_KV_LORA = 128
MLA_SCALE = (MLA_NOPE + MLA_ROPE) ** -0.5
QBLOCK = 128
DIL_GROUPS = ((128, 1), (512, 4), (2048, 16))
DIL_N_GROUPS = len(DIL_GROUPS)
DIL_HEADS = 8
DIL_HEAD_DIM = 64
DIL_BLOCK = 128
D_FF = 2816
N_EXPERTS = 8
TOP_K = 2
D_FF_EXPERT = 3584
ROPE_THETA = 10000.0
LN_EPS = 1e-5
RMS_EPS = 1e-6
N_MLA = (DEPTH + 1) // 2
N_DIL = DEPTH // 2
DEEPNORM_ALPHA = (2 * DEPTH) ** 0.25
DEEPNORM_BETA = (8 * DEPTH) ** -0.25

kernel_name = 'hybrid_mla_dilated_deepnorm_moe_step'


def layer_norm(x, g, b):
    xf = x.astype(jnp.float32)
    mu = jnp.mean(xf, -1, keepdims=True)
    var = jnp.mean(jnp.square(xf - mu), -1, keepdims=True)
    return ((xf - mu) * lax.rsqrt(var + LN_EPS) * g + b).astype(x.dtype)


def rms_norm(x, g):
    xf = x.astype(jnp.float32)
    return (xf * lax.rsqrt(jnp.mean(xf * xf, -1, keepdims=True) + RMS_EPS) * g).astype(x.dtype)


def rope(x, pos):
    half = x.shape[-1] // 2
    inv_freq = ROPE_THETA ** (-jnp.arange(half, dtype=jnp.float32) / half)
    ang = pos.astype(jnp.float32)[:, None] * inv_freq[None, :]
    cos = jnp.cos(ang)[:, None, :]
    sin = jnp.sin(ang)[:, None, :]
    xf = x.astype(jnp.float32)
    x1, x2 = xf[..., :half], xf[..., half:]
    return jnp.concatenate([x1 * cos - x2 * sin, x2 * cos + x1 * sin], -1).astype(x.dtype)


def mla_project(x, pos, w_in, q_norm, w_uq, kv_norm):
    B, S, _ = x.shape
    h = x @ w_in
    c_q = rms_norm(h[..., :MLA_Q_LORA], q_norm)
    c_kv = rms_norm(h[..., MLA_Q_LORA:MLA_Q_LORA + MLA_KV_LORA], kv_norm)
    k_pe = rope(h[..., MLA_Q_LORA + MLA_KV_LORA:][:, :, None, :], pos)[:, :, 0, :]
    q = (c_q @ w_uq).reshape(B, S, MLA_HEADS, MLA_NOPE + MLA_ROPE)
    q_nope = q[..., :MLA_NOPE]
    q_pe = rope(q[..., MLA_NOPE:], pos)
    return q_nope, q_pe, c_kv, k_pe


def mla_prompt(x, w_in, q_norm, w_uq, kv_norm, w_uk, w_uv, w_o):
    B, S, _ = x.shape
    q_nope, q_pe, c_kv, k_pe = mla_project(x, jnp.arange(S), w_in, q_norm, w_uq, kv_norm)
    k_nope = jnp.einsum('bsc,chd->bshd', c_kv, w_uk)
    v = jnp.einsum('bsc,chd->bshd', c_kv, w_uv)
    nq = S // QBLOCK
    qn_b = q_nope.reshape(B, nq, QBLOCK, MLA_HEADS, MLA_NOPE).swapaxes(0, 1)
    qp_b = q_pe.reshape(B, nq, QBLOCK, MLA_HEADS, MLA_ROPE).swapaxes(0, 1)
    starts = jnp.arange(nq, dtype=jnp.int32) * QBLOCK
    kpos = jnp.arange(S)

    def attend(block):
        qn, qp, start = block
        s = (jnp.einsum('bqhd,bkhd->bhqk', qn, k_nope).astype(jnp.float32)
             + jnp.einsum('bqhd,bkd->bhqk', qp, k_pe).astype(jnp.float32)) * MLA_SCALE
        qpos = start + jnp.arange(QBLOCK)
        s = jnp.where(kpos[None, :] <= qpos[:, None], s, -jnp.inf)
        p = jax.nn.softmax(s, axis=-1).astype(v.dtype)
        return jnp.einsum('bhqk,bkhd->bqhd', p, v)

    o = lax.map(attend, (qn_b, qp_b, starts))
    o = o.swapaxes(0, 1).reshape(B, S, MLA_HEADS * MLA_VDIM)
    return o @ w_o, c_kv, k_pe


def mla_sample(x, ckv_pool, kpe_pool, page_table, w_in, q_norm, w_uq, kv_norm, w_uk, w_uv, w_o):
    Bd, T, _ = x.shape
    past = page_table.shape[1] * PAGE_SIZE
    qpos = past + jnp.arange(T)
    q_nope, q_pe, c_kv, k_pe = mla_project(x, qpos, w_in, q_norm, w_uq, kv_norm)
    past_ckv = ckv_pool[page_table].reshape(Bd, past, MLA_KV_LORA)
    past_kpe = kpe_pool[page_table].reshape(Bd, past, MLA_ROPE)
    ckv_all = jnp.concatenate([past_ckv, c_kv.astype(past_ckv.dtype)], 1)
    kpe_all = jnp.concatenate([past_kpe, k_pe.astype(past_kpe.dtype)], 1)
    q_lat = jnp.einsum('bthd,chd->bthc', q_nope, w_uk)
    s = (jnp.einsum('bthc,bkc->bhtk', q_lat, ckv_all).astype(jnp.float32)
         + jnp.einsum('bthd,bkd->bhtk', q_pe, kpe_all).astype(jnp.float32)) * MLA_SCALE
    kpos = jnp.arange(past + T)
    s = jnp.where((kpos[None, :] <= qpos[:, None])[None, None], s, -jnp.inf)
    p = jax.nn.softmax(s, axis=-1).astype(ckv_all.dtype)
    o_lat = jnp.einsum('bhtk,bkc->bthc', p, ckv_all)
    o = jnp.einsum('bthc,chd->bthd', o_lat, w_uv).reshape(Bd, T, MLA_HEADS * MLA_VDIM)
    return o @ w_o, c_kv, k_pe


def dil_project(x, pos, w_in):
    B, S, _ = x.shape
    h = (x @ w_in).reshape(B, S, DIL_N_GROUPS, 3, DIL_HEADS, DIL_HEAD_DIM)
    n_all = DIL_N_GROUPS * DIL_HEADS
    shp = (B, S, DIL_N_GROUPS, DIL_HEADS, DIL_HEAD_DIM)
    q = rope(h[:, :, :, 0].reshape(B, S, n_all, DIL_HEAD_DIM), pos).reshape(shp)
    k = rope(h[:, :, :, 1].reshape(B, S, n_all, DIL_HEAD_DIM), pos).reshape(shp)
    v = h[:, :, :, 2]
    return q, k, v


def dilated_group_prompt(q, k, v, window, dilation):
    B, S, H, E = q.shape
    span = dilation * DIL_BLOCK
    s_pad = ((S + span - 1) // span) * span
    n_sub = s_pad // dilation
    nb = n_sub // DIL_BLOCK
    reach = window // dilation

    def to_blocks(t):
        t = jnp.pad(t, ((0, 0), (0, s_pad - S), (0, 0), (0, 0)))
        t = t.reshape(B, n_sub, dilation, H, E).transpose(0, 2, 1, 3, 4)
        return t.reshape(B, dilation, nb, DIL_BLOCK, H, E)

    def with_prev(t):
        prev = jnp.pad(t[:, :, :-1], ((0, 0), (0, 0), (1, 0), (0, 0), (0, 0), (0, 0)))
        return jnp.concatenate([prev, t], axis=3)

    qb = to_blocks(q)
    kb = with_prev(to_blocks(k))
    vb = with_prev(to_blocks(v))
    s = jnp.einsum('brnqhe,brnkhe->brnhqk', qb, kb).astype(jnp.float32) * (E ** -0.5)
    qi = jnp.arange(DIL_BLOCK)
    ki = jnp.arange(2 * DIL_BLOCK) - DIL_BLOCK
    step = qi[:, None] - ki[None, :]
    exists = (jnp.arange(nb)[:, None, None] * DIL_BLOCK + ki[None, None, :]) >= 0
    valid = (step >= 0) & (step <= reach) & exists
    s = jnp.where(valid[None, None, :, None], s, -jnp.inf)
    m = jnp.max(s, -1, keepdims=True)
    p = jnp.exp(s - m)
    den = jnp.sum(p, -1)
    o = jnp.einsum('brnhqk,brnkhe->brnqhe', p, vb.astype(jnp.float32))
    o = o / jnp.moveaxis(den, -1, -2)[..., None]
    lse = jnp.moveaxis(m[..., 0] + jnp.log(den), -1, -2)
    o = o.reshape(B, dilation, n_sub, H, E).transpose(0, 2, 1, 3, 4).reshape(B, s_pad, H, E)[:, :S]
    lse = lse.reshape(B, dilation, n_sub, H).transpose(0, 2, 1, 3).reshape(B, s_pad, H)[:, :S]
    return o, lse


def dilated_group_sample(q, k_all, v_all, n_buf, window, dilation):
    E = q.shape[-1]
    T = q.shape[1]
    n_keys = window // dilation + 1
    idx = n_buf + jnp.arange(T)[:, None] - dilation * jnp.arange(n_keys)[None, :]
    valid = idx >= 0
    idx = jnp.maximum(idx, 0)
    kg = k_all[:, idx]
    vg = v_all[:, idx]
    s = jnp.einsum('bthe,btjhe->bthj', q, kg).astype(jnp.float32) * (E ** -0.5)
    s = jnp.where(valid[None, :, None, :], s, -jnp.inf)
    m = jnp.max(s, -1, keepdims=True)
    p = jnp.exp(s - m)
    den = jnp.sum(p, -1)
    o = jnp.einsum('bthj,btjhe->bthe', p, vg.astype(jnp.float32)) / den[..., None]
    return o, m[..., 0] + jnp.log(den)


def combine_groups(outs, lses):
    wts = jax.nn.softmax(jnp.stack(lses, 0), axis=0)
    return jnp.sum(wts[..., None] * jnp.stack(outs, 0), axis=0)


def dil_prompt(x, w_in, w_o):
    B, S, _ = x.shape
    q, k, v = dil_project(x, jnp.arange(S), w_in)
    outs, lses, states = [], [], []
    for g, (window, dilation) in enumerate(DIL_GROUPS):
        o, l = dilated_group_prompt(q[:, :, g], k[:, :, g], v[:, :, g], window, dilation)
        outs.append(o)
        lses.append(l)
        keep = min(window, S)
        states.append(jnp.stack([k[:, S - keep:, g], v[:, S - keep:, g]], axis=2))
    o = combine_groups(outs, lses).astype(x.dtype).reshape(B, S, DIL_HEADS * DIL_HEAD_DIM)
    return o @ w_o, states


def dil_sample(x, bufs, past, w_in, w_o):
    Bd, T, _ = x.shape
    q, k, v = dil_project(x, past + jnp.arange(T), w_in)
    outs, lses, states = [], [], []
    for g, ((window, dilation), buf) in enumerate(zip(DIL_GROUPS, bufs)):
        n_buf = buf.shape[1]
        k_all = jnp.concatenate([buf[:, :, 0], k[:, :, g].astype(buf.dtype)], 1)
        v_all = jnp.concatenate([buf[:, :, 1], v[:, :, g].astype(buf.dtype)], 1)
        o, l = dilated_group_sample(q[:, :, g], k_all, v_all, n_buf, window, dilation)
        outs.append(o)
        lses.append(l)
        keep = min(window, n_buf + T)
        states.append(jnp.stack([k_all[:, n_buf + T - keep:], v_all[:, n_buf + T - keep:]], axis=2))
    o = combine_groups(outs, lses).astype(x.dtype).reshape(Bd, T, DIL_HEADS * DIL_HEAD_DIM)
    return o @ w_o, states


def swiglu(x, w_gate_up, w_down):
    g, u = jnp.split(x @ w_gate_up, 2, axis=-1)
    return (jax.nn.silu(g) * u) @ w_down


def moe_ffn(x, w_router, b_router, w_gate_up, w_down):
    shp = x.shape
    xt = x.reshape(-1, shp[-1])
    logits = (xt @ w_router).astype(jnp.float32) + b_router.astype(jnp.float32)
    top_val, top_idx = lax.top_k(logits, TOP_K)
    gates = jax.nn.softmax(top_val, axis=-1)
    combine = jnp.sum(jax.nn.one_hot(top_idx, N_EXPERTS, dtype=jnp.float32) * gates[..., None], axis=1)
    y = jnp.zeros(xt.shape, jnp.float32)
    for e in range(N_EXPERTS):
        y = y + combine[:, e:e + 1] * swiglu(xt, w_gate_up[e], w_down[e]).astype(jnp.float32)
    return y.astype(x.dtype).reshape(shp)


def setup_inputs(seed: int = 0) -> dict:
    key = jax.random.key(seed)
    keys = iter(jax.random.split(key, 40))

    def normal(shape, scale):
        return jax.random.normal(next(keys), shape, jnp.float32) * scale

    n_pages = PAST_LEN // PAGE_SIZE
    n_phys = (5 * DEC_BATCH * n_pages) // 4
    page_table = jax.random.permutation(next(keys), n_phys)[:DEC_BATCH * n_pages]
    page_table = page_table.reshape(DEC_BATCH, n_pages).astype(jnp.int32)
    dil_width = DIL_HEADS * DIL_HEAD_DIM
    mla_width = MLA_HEADS * MLA_VDIM
    win_shape = lambda w: (N_DIL, DEC_BATCH, min(w, PAST_LEN), 2, DIL_HEADS, DIL_HEAD_DIM)
    return {
        'x_prompt': normal((BATCH, SEQ, D_MODEL), 1.0),
        'x_sample': normal((DEC_BATCH, DEC_SEQ, D_MODEL), 1.0),
        'cache_mla_ckv': normal((N_MLA, n_phys, PAGE_SIZE, MLA_KV_LORA), 1.0),
        'cache_mla_kpe': normal((N_MLA, n_phys, PAGE_SIZE, MLA_ROPE), 1.0),
        'cache_dil_w128': normal(win_shape(DIL_GROUPS[0][0]), 1.0),
        'cache_dil_w512': normal(win_shape(DIL_GROUPS[1][0]), 1.0),
        'cache_dil_w2048': normal(win_shape(DIL_GROUPS[2][0]), 1.0),
        'page_table': page_table,
        'mla_w_in': normal((N_MLA, D_MODEL, MLA_Q_LORA + MLA_KV_LORA + MLA_ROPE), D_MODEL ** -0.5),
        'mla_q_norm': 1.0 + normal((N_MLA, MLA_Q_LORA), 0.02),
        'mla_w_uq': normal((N_MLA, MLA_Q_LORA, MLA_HEADS * (MLA_NOPE + MLA_ROPE)), MLA_Q_LORA ** -0.5),
        'mla_kv_norm': 1.0 + normal((N_MLA, MLA_KV_LORA), 0.02),
        'mla_w_uk': normal((N_MLA, MLA_KV_LORA, MLA_HEADS, MLA_NOPE), MLA_KV_LORA ** -0.5),
        'mla_w_uv': normal((N_MLA, MLA_KV_LORA, MLA_HEADS, MLA_VDIM), MLA_KV_LORA ** -0.5),
        'mla_w_o': normal((N_MLA, mla_width, D_MODEL), mla_width ** -0.5 * DEEPNORM_BETA),
        'dil_w_in': normal((N_DIL, D_MODEL, DIL_N_GROUPS * 3 * dil_width), D_MODEL ** -0.5),
        'dil_w_o': normal((N_DIL, dil_width, D_MODEL), dil_width ** -0.5 * DEEPNORM_BETA),
        'ffn_w_gate_up': normal((N_MLA, D_MODEL, 2 * D_FF), D_MODEL ** -0.5),
        'ffn_w_down': normal((N_MLA, D_FF, D_MODEL), D_FF ** -0.5 * DEEPNORM_BETA),
        'moe_w_router': normal((N_DIL, D_MODEL, N_EXPERTS), D_MODEL ** -0.5),
        'moe_b_router': normal((N_DIL, N_EXPERTS), 0.01),
        'moe_w_gate_up': normal((N_DIL, N_EXPERTS, D_MODEL, 2 * D_FF_EXPERT), D_MODEL ** -0.5),
        'moe_w_down': normal((N_DIL, N_EXPERTS, D_FF_EXPERT, D_MODEL), D_FF_EXPERT ** -0.5 * DEEPNORM_BETA),
        'ln_mix_g': 1.0 + normal((DEPTH, D_MODEL), 0.02),
        'ln_mix_b': normal((DEPTH, D_MODEL), 0.02),
        'ln_ffn_g': 1.0 + normal((DEPTH, D_MODEL), 0.02),
        'ln_ffn_b': normal((DEPTH, D_MODEL), 0.02),
    }


def reference(x_prompt, x_sample, cache_mla_ckv, cache_mla_kpe, cache_dil_w128, cache_dil_w512,
              cache_dil_w2048, page_table, mla_w_in, mla_q_norm, mla_w_uq, mla_kv_norm, mla_w_uk,
              mla_w_uv, mla_w_o, dil_w_in, dil_w_o, ffn_w_gate_up, ffn_w_down, moe_w_router,
              moe_b_router, moe_w_gate_up, moe_w_down, ln_mix_g, ln_mix_b, ln_ffn_g, ln_ffn_b):
    past = page_table.shape[1] * PAGE_SIZE
    dil_bufs = (cache_dil_w128, cache_dil_w512, cache_dil_w2048)
    xp, xs = x_prompt, x_sample
    ckv_p, kpe_p, ckv_s, kpe_s = [], [], [], []
    win_p = [[] for _ in DIL_GROUPS]
    win_s = [[] for _ in DIL_GROUPS]
    for i in range(DEPTH):
        li = i // 2
        if i % 2 == 0:
            w = (mla_w_in[li], mla_q_norm[li], mla_w_uq[li], mla_kv_norm[li], mla_w_uk[li], mla_w_uv[li], mla_w_o[li])
            mp, c_p, r_p = mla_prompt(xp, *w)
            ms, c_s, r_s = mla_sample(xs, cache_mla_ckv[li], cache_mla_kpe[li], page_table, *w)
            ckv_p.append(c_p)
            kpe_p.append(r_p)
            ckv_s.append(c_s)
            kpe_s.append(r_s)
        else:
            mp, st_p = dil_prompt(xp, dil_w_in[li], dil_w_o[li])
            ms, st_s = dil_sample(xs, tuple(b[li] for b in dil_bufs), past, dil_w_in[li], dil_w_o[li])
            for g in range(DIL_N_GROUPS):
                win_p[g].append(st_p[g])
                win_s[g].append(st_s[g])
        xp = layer_norm(DEEPNORM_ALPHA * xp + mp, ln_mix_g[i], ln_mix_b[i])
        xs = layer_norm(DEEPNORM_ALPHA * xs + ms, ln_mix_g[i], ln_mix_b[i])
        if i % 2 == 0:
            fp = swiglu(xp, ffn_w_gate_up[li], ffn_w_down[li])
            fs = swiglu(xs, ffn_w_gate_up[li], ffn_w_down[li])
        else:
            fp = moe_ffn(xp, moe_w_router[li], moe_b_router[li], moe_w_gate_up[li], moe_w_down[li])
            fs = moe_ffn(xs, moe_w_router[li], moe_b_router[li], moe_w_gate_up[li], moe_w_down[li])
        xp = layer_norm(DEEPNORM_ALPHA * xp + fp, ln_ffn_g[i], ln_ffn_b[i])
        xs = layer_norm(DEEPNORM_ALPHA * xs + fs, ln_ffn_g[i], ln_ffn_b[i])
    y_prompt, y_sample = xp, xs
    new_ckv_prompt = jnp.stack(ckv_p, 0)
    new_kpe_prompt = jnp.stack(kpe_p, 0)
    new_ckv_sample = jnp.stack(ckv_s, 0)
    new_kpe_sample = jnp.stack(kpe_s, 0)
    new_w128_prompt = jnp.stack(win_p[0], 0)
    new_w512_prompt = jnp.stack(win_p[1], 0)
    new_w2048_prompt = jnp.stack(win_p[2], 0)
    new_w128_sample = jnp.stack(win_s[0], 0)
    new_w512_sample = jnp.stack(win_s[1], 0)
    new_w2048_sample = jnp.stack(win_s[2], 0)
    return (y_prompt, y_sample, new_ckv_prompt, new_kpe_prompt, new_ckv_sample, new_kpe_sample,
            new_w128_prompt, new_w512_prompt, new_w2048_prompt,
            new_w128_sample, new_w512_sample, new_w2048_sample)
```

```python
import functools

import jax
import jax.numpy as jnp
from jax import lax
from jax.experimental import pallas as pl
from jax.experimental.pallas import tpu as pltpu

F32 = jnp.float32
BF16 = jnp.bfloat16

D_MODEL = 1024
DEPTH = 4
PAGE_SIZE = 128
MLA_HEADS = 16
MLA_NOPE = 64
MLA_ROPE = 32
MLA_VDIM = 64
MLA_Q_LORA = 256
MLA_KV_LORA = 128
MLA_SCALE = (MLA_NOPE + MLA_ROPE) ** -0.5
DIL_GROUPS = ((128, 1), (512, 4), (2048, 16))
DIL_HEADS = 8
DIL_HEAD_DIM = 64
DIL_BLOCK = 128
DIL_WIDTH = DIL_HEADS * DIL_HEAD_DIM
N_EXPERTS = 8
ROPE_THETA = 10000.0
LN_EPS = 1e-5
RMS_EPS = 1e-6
DEEPNORM_ALPHA = (2 * DEPTH) ** 0.25

LANES = 128
V7X_VMEM_BUDGET = 56 * 1024 * 1024


def _params(semantics, vmem_bytes):
    return pltpu.CompilerParams(dimension_semantics=semantics,
                                vmem_limit_bytes=min(int(vmem_bytes), V7X_VMEM_BUDGET))


def _dot(a, b):
    return jnp.dot(a, b, preferred_element_type=F32)


def _dot_nt(a, b):
    return lax.dot_general(a, b, (((1,), (1,)), ((), ())), preferred_element_type=F32)


def _layer_norm(z, g, b):
    mu = jnp.mean(z, -1, keepdims=True)
    zc = z - mu
    var = jnp.mean(zc * zc, -1, keepdims=True)
    return zc * lax.rsqrt(var + LN_EPS) * g + b


def _rms_norm(h, g):
    return h * lax.rsqrt(jnp.mean(h * h, -1, keepdims=True) + RMS_EPS) * g


def _rope(x, tab_ref, half):
    n = x.shape[-1]
    return (x * tab_ref[0] + pltpu.roll(x, half, 1) * tab_ref[1]
            + pltpu.roll(x, n - half, 1) * tab_ref[2])


def _rope_table_kernel(pos_ref, freq_ref, tab_ref, *, half):
    ang = pos_ref[...] * freq_ref[...]
    c = jnp.cos(ang)
    s = jnp.sin(ang)
    lane = lax.broadcasted_iota(jnp.int32, ang.shape, 1)
    upper = (lane % (2 * half)) >= half
    tab_ref[0] = c
    tab_ref[1] = jnp.where(upper, s, 0.0)
    tab_ref[2] = jnp.where(upper, 0.0, -s)


def rope_tables(pos, half):
    p = pos.shape[0]
    lane = jnp.arange(LANES)
    freq = (ROPE_THETA ** (-(lane % half).astype(F32) / half)).reshape(1, LANES)
    return pl.pallas_call(
        functools.partial(_rope_table_kernel, half=half),
        out_shape=jax.ShapeDtypeStruct((3, p, LANES), F32),
        name=f"rope_tables_{half}",
    )(pos.astype(F32).reshape(p, 1), freq)


def _mla_proj_kernel(x_ref, win_ref, qn_ref, kvn_ref, wuq_ref, wukt_ref, tab_ref,
                     ckv_ref, kpe_ref, kfull_ref, q_ref):
    xb = x_ref[...].astype(BF16)
    h = _dot(xb, win_ref[...])
    c_q = _rms_norm(h[:, :MLA_Q_LORA], qn_ref[...])
    c_kv = _rms_norm(h[:, MLA_Q_LORA:MLA_Q_LORA + MLA_KV_LORA], kvn_ref[...])
    kpe = _rope(h[:, MLA_Q_LORA + MLA_KV_LORA:], tab_ref, MLA_ROPE // 2)
    ckv_ref[...] = c_kv
    kpe_ref[...] = kpe
    kfull_ref[...] = jnp.concatenate([c_kv, kpe], -1).astype(BF16)
    q = _dot(c_q.astype(BF16), wuq_ref[...])
    lane = lax.broadcasted_iota(jnp.int32, (q.shape[0], LANES), 1)
    for hd in range(MLA_HEADS):
        qc = q[:, hd * LANES:(hd + 1) * LANES]
        q_lat = _dot(qc.astype(BF16), wukt_ref[hd])
        q_pe = _rope(pltpu.roll(qc, LANES - MLA_NOPE, 1), tab_ref, MLA_ROPE // 2)
        q_pe = jnp.where(lane < MLA_ROPE, q_pe, 0.0)
        q_ref[hd] = jnp.concatenate([q_lat, q_pe], -1).astype(q_ref.dtype)


def mla_project(x, w, tab, tab_block, q_dtype, tm):
    n = x.shape[0]
    const = lambda *shape: pl.BlockSpec(shape, lambda i: (0,) * len(shape))
    return pl.pallas_call(
        _mla_proj_kernel,
        grid=(n // tm,),
        in_specs=[
            pl.BlockSpec((tm, D_MODEL), lambda i: (i, 0)),
            const(D_MODEL, 512), const(1, MLA_Q_LORA), const(1, MLA_KV_LORA),
            const(MLA_Q_LORA, MLA_HEADS * LANES), const(MLA_HEADS, LANES, LANES),
            pl.BlockSpec((3, tm, LANES), lambda i: (0, tab_block(i), 0)),
        ],
        out_specs=[
            pl.BlockSpec((tm, LANES), lambda i: (i, 0)),
            pl.BlockSpec((tm, LANES), lambda i: (i, 0)),
            pl.BlockSpec((tm, 2 * LANES), lambda i: (i, 0)),
            pl.BlockSpec((MLA_HEADS, tm, 2 * LANES), lambda i: (0, i, 0)),
        ],
        out_shape=[
            jax.ShapeDtypeStruct((n, LANES), F32),
            jax.ShapeDtypeStruct((n, LANES), F32),
            jax.ShapeDtypeStruct((n, 2 * LANES), BF16),
            jax.ShapeDtypeStruct((MLA_HEADS, n, 2 * LANES), q_dtype),
        ],
        compiler_params=_params(("arbitrary",), 48 << 20),
        name="mla_project",
    )(x, w["w_in"], w["q_norm"], w["kv_norm"], w["w_uq"], w["w_ukt"], tab)


def _mla_prompt_attn_kernel(q_ref, k_ref, o_ref, m_sc, l_sc, acc_sc, *, tq, tk):
    iq = pl.program_id(1)
    rows = q_ref.shape[0] * tq
    q = q_ref[...].reshape(rows, q_ref.shape[2])
    m_sc[...] = jnp.full(m_sc.shape, -jnp.inf, F32)
    l_sc[...] = jnp.zeros(l_sc.shape, F32)
    acc_sc[...] = jnp.zeros(acc_sc.shape, F32)

    def step(kt, masked):
        k = k_ref[pl.ds(pl.multiple_of(kt * tk, tk), tk), :]
        s = _dot_nt(q, k) * MLA_SCALE
        if masked:
            qpos = iq * tq + lax.broadcasted_iota(jnp.int32, s.shape, 0) % tq
            kpos = kt * tk + lax.broadcasted_iota(jnp.int32, s.shape, 1)
            s = jnp.where(kpos <= qpos, s, -jnp.inf)
        m_prev = m_sc[...]
        m_new = jnp.maximum(m_prev, jnp.max(s, -1, keepdims=True))
        alpha = jnp.exp(m_prev - m_new)
        p = jnp.exp(s - m_new)
        l_sc[...] = alpha * l_sc[...] + jnp.sum(p, -1, keepdims=True)
        acc_sc[...] = alpha * acc_sc[...] + _dot(p.astype(BF16), k[:, :MLA_KV_LORA])
        m_sc[...] = m_new

    n_full = (iq * tq + 1) // tk
    n_all = (iq * tq + tq - 1) // tk + 1
    lax.fori_loop(0, n_full, lambda kt, c: (step(kt, False), c)[1], 0)
    lax.fori_loop(n_full, n_all, lambda kt, c: (step(kt, True), c)[1], 0)
    o = acc_sc[...] / l_sc[...]
    o_ref[...] = o.reshape(o_ref.shape).astype(o_ref.dtype)


def mla_prompt_attention(q, kfull, batch, seq, tq=128, tk=256):
    heads, n, dq = q.shape
    nq = seq // tq
    rows = heads * tq
    return pl.pallas_call(
        functools.partial(_mla_prompt_attn_kernel, tq=tq, tk=tk),
        grid=(batch, nq),
        in_specs=[
            pl.BlockSpec((heads, tq, dq), lambda b, i: (0, b * nq + i, 0)),
            pl.BlockSpec((seq, dq), lambda b, i: (b, 0)),
        ],
        out_specs=pl.BlockSpec((heads, tq, MLA_KV_LORA), lambda b, i: (0, b * nq + i, 0)),
        out_shape=jax.ShapeDtypeStruct((heads, n, MLA_KV_LORA), BF16),
        scratch_shapes=[pltpu.VMEM((rows, 1), F32), pltpu.VMEM((rows, 1), F32),
                        pltpu.VMEM((rows, MLA_KV_LORA), F32)],
        compiler_params=_params(("arbitrary", "arbitrary"), 40 << 20),
        name="mla_prompt_attention",
    )(q, kfull)


def _mla_sample_attn_kernel(pt_ref, q_ref, cnew_ref, rnew_ref, *rest, group, pages, t_new):
    n_pg = group * pages
    ckv_refs = rest[:n_pg]
    kpe_refs = rest[n_pg:2 * n_pg]
    o_ref, m_sc, l_sc, acc_sc = rest[2 * n_pg:]
    j = pl.program_id(1)
    rows = q_ref.shape[0] * q_ref.shape[1]
    q = q_ref[...].reshape(rows, q_ref.shape[2])
    q_lat = q[:, :MLA_KV_LORA].astype(BF16)
    q_pe = q[:, MLA_KV_LORA:MLA_KV_LORA + MLA_ROPE].astype(BF16)

    @pl.when(j == 0)
    def _():
        m_sc[...] = jnp.full(m_sc.shape, -jnp.inf, F32)
        l_sc[...] = jnp.zeros(l_sc.shape, F32)
        acc_sc[...] = jnp.zeros(acc_sc.shape, F32)

    def update(bb, s, v):
        m_prev = m_sc[bb]
        m_new = jnp.maximum(m_prev, jnp.max(s, -1, keepdims=True))
        alpha = jnp.exp(m_prev - m_new)
        p = jnp.exp(s - m_new)
        l_sc[bb] = alpha * l_sc[bb] + jnp.sum(p, -1, keepdims=True)
        acc_sc[bb] = alpha * acc_sc[bb] + _dot(p.astype(BF16), v)
        m_sc[bb] = m_new

    for bb in range(group):
        ck = jnp.concatenate([ckv_refs[bb * pages + p][...] for p in range(pages)], 0).astype(BF16)
        kp = jnp.concatenate([kpe_refs[bb * pages + p][...] for p in range(pages)], 0).astype(BF16)
        s = (_dot_nt(q_lat, ck) + _dot_nt(q_pe, kp)) * MLA_SCALE
        update(bb, s, ck)

    @pl.when(j == pl.num_programs(1) - 1)
    def _():
        cn = cnew_ref[...].astype(BF16)
        rn = rnew_ref[...][:, :MLA_ROPE].astype(BF16)
        s_new = (_dot_nt(q_lat, cn) + _dot_nt(q_pe, rn)) * MLA_SCALE
        tok_q = lax.broadcasted_iota(jnp.int32, s_new.shape, 0) % (group * t_new)
        tok_k = lax.broadcasted_iota(jnp.int32, s_new.shape, 1)
        out = jnp.zeros(acc_sc.shape[1:], F32)
        for bb in range(group):
            ok = (tok_k // t_new == bb) & (tok_k % t_new <= tok_q % t_new)
            update(bb, jnp.where(ok, s_new, -jnp.inf), cn)
            sel = lax.broadcasted_iota(jnp.int32, out.shape, 0) % (group * t_new) // t_new == bb
            out = jnp.where(sel, acc_sc[bb] / l_sc[bb], out)
        o_ref[...] = out.reshape(o_ref.shape)


def mla_sample_attention(q, ckv_new, kpe_new, ckv_pool, kpe_pool, page_table, layer, t_new,
                         group=2, pages=4):
    heads, n, dq = q.shape
    n_batch, n_pages = page_table.shape
    gt = group * t_new
    rows = heads * gt

    def pool_spec(bb, p, width):
        def index(g, j, pt):
            return (layer, pt[(g * group + bb) * n_pages + j * pages + p], 0, 0)
        return pl.BlockSpec((None, None, PAGE_SIZE, width), index)

    in_specs = [
        pl.BlockSpec((heads, gt, dq), lambda g, j, pt: (0, g, 0)),
        pl.BlockSpec((gt, LANES), lambda g, j, pt: (g, 0)),
        pl.BlockSpec((gt, LANES), lambda g, j, pt: (g, 0)),
    ]
    in_specs += [pool_spec(bb, p, MLA_KV_LORA) for bb in range(group) for p in range(pages)]
    in_specs += [pool_spec(bb, p, MLA_ROPE) for bb in range(group) for p in range(pages)]
    grid_spec = pltpu.PrefetchScalarGridSpec(
        num_scalar_prefetch=1,
        grid=(n_batch // group, n_pages // pages),
        in_specs=in_specs,
        out_specs=pl.BlockSpec((heads, gt, MLA_KV_LORA), lambda g, j, pt: (0, g, 0)),
        scratch_shapes=[pltpu.VMEM((group, rows, 1), F32), pltpu.VMEM((group, rows, 1), F32),
                        pltpu.VMEM((group, rows, MLA_KV_LORA), F32)],
    )
    n_pg = group * pages
    return pl.pallas_call(
        functools.partial(_mla_sample_attn_kernel, group=group, pages=pages, t_new=t_new),
        grid_spec=grid_spec,
        out_shape=jax.ShapeDtypeStruct((heads, n, MLA_KV_LORA), F32),
        compiler_params=_params(("arbitrary", "arbitrary"), 32 << 20),
        name="mla_sample_attention",
    )(page_table.reshape(-1), q, ckv_new, kpe_new, *([ckv_pool] * n_pg), *([kpe_pool] * n_pg))


def _mla_out_kernel(o_ref, wuv_ref, wo_ref, x_ref, g_ref, b_ref, y_ref):
    parts = []
    for p in range(MLA_HEADS // 2):
        a = jnp.concatenate([o_ref[2 * p], o_ref[2 * p + 1]], -1).astype(BF16)
        parts.append(_dot(a, wuv_ref[p]).astype(BF16))
    v = jnp.concatenate(parts, -1)
    z = DEEPNORM_ALPHA * x_ref[...] + _dot(v, wo_ref[...])
    y_ref[...] = _layer_norm(z, g_ref[...], b_ref[...])


def mla_output(o_lat, w, x, g, b, tm):
    heads, n, c = o_lat.shape
    const = lambda *shape: pl.BlockSpec(shape, lambda i: (0,) * len(shape))
    return pl.pallas_call(
        _mla_out_kernel,
        grid=(n // tm,),
        in_specs=[
            pl.BlockSpec((heads, tm, c), lambda i: (0, i, 0)),
            const(heads // 2, 2 * c, LANES), const(heads * MLA_VDIM, D_MODEL),
            pl.BlockSpec((tm, D_MODEL), lambda i: (i, 0)),
            const(1, D_MODEL), const(1, D_MODEL),
        ],
        out_specs=pl.BlockSpec((tm, D_MODEL), lambda i: (i, 0)),
        out_shape=jax.ShapeDtypeStruct((n, D_MODEL), F32),
        compiler_params=_params(("arbitrary",), 40 << 20),
        name="mla_output",
    )(o_lat, w["w_uv_pair"], w["w_o"], x, g, b)


def _proj_norm_kernel(a_ref, wo_ref, x_ref, g_ref, b_ref, y_ref):
    z = DEEPNORM_ALPHA * x_ref[...] + _dot(a_ref[...].astype(BF16), wo_ref[...])
    y_ref[...] = _layer_norm(z, g_ref[...], b_ref[...])


def proj_norm(a, w_o, x, g, b, tm):
    n, k = a.shape
    const = lambda *shape: pl.BlockSpec(shape, lambda i: (0,) * len(shape))
    return pl.pallas_call(
        _proj_norm_kernel,
        grid=(n // tm,),
        in_specs=[
            pl.BlockSpec((tm, k), lambda i: (i, 0)), const(k, D_MODEL),
            pl.BlockSpec((tm, D_MODEL), lambda i: (i, 0)), const(1, D_MODEL), const(1, D_MODEL),
        ],
        out_specs=pl.BlockSpec((tm, D_MODEL), lambda i: (i, 0)),
        out_shape=jax.ShapeDtypeStruct((n, D_MODEL), F32),
        compiler_params=_params(("arbitrary",), 32 << 20),
        name="proj_norm",
    )(a, w_o, x, g, b)


def _top2_combine(x, wr_ref, br_ref, n_exp):
    x1 = x.astype(BF16)
    x2 = (x - x1.astype(F32)).astype(BF16)
    logits = (_dot(x1, wr_ref[0]) + _dot(x1, wr_ref[1]) + _dot(x2, wr_ref[0])
              + _dot(x2, wr_ref[1]) + br_ref[...])
    lane = lax.broadcasted_iota(jnp.int32, logits.shape, 1)
    logits = jnp.where(lane < n_exp, logits, -jnp.inf)
    v1 = jnp.max(logits, -1, keepdims=True)
    i1 = jnp.min(jnp.where(logits == v1, lane, LANES), -1, keepdims=True)
    rest = jnp.where(lane == i1, -jnp.inf, logits)
    v2 = jnp.max(rest, -1, keepdims=True)
    i2 = jnp.min(jnp.where(rest == v2, lane, LANES), -1, keepdims=True)
    e2 = jnp.exp(v2 - v1)
    den = 1.0 + e2
    return jnp.where(lane == i1, 1.0 / den, 0.0) + jnp.where(lane == i2, e2 / den, 0.0)


def _ffn_kernel(x_ref, wr_ref, br_ref, wg_ref, wu_ref, wd_ref, g_ref, b_ref, y_ref,
                xb_sc, acc_sc, mix_sc, comb_sc, *, n_exp):
    e = pl.program_id(1)
    f = pl.program_id(2)
    last_f = pl.num_programs(2) - 1

    @pl.when((e == 0) & (f == 0))
    def _():
        x = x_ref[...]
        xb_sc[...] = x.astype(BF16)
        if n_exp > 1:
            mix_sc[...] = jnp.zeros(mix_sc.shape, F32)
            comb_sc[...] = _top2_combine(x, wr_ref, br_ref, n_exp)

    xb = xb_sc[...]
    hg = _dot(xb, wg_ref[...])
    hu = _dot(xb, wu_ref[...])
    act = (hg / (1.0 + jnp.exp(-hg)) * hu).astype(BF16)
    part = _dot(act, wd_ref[...])

    @pl.when(f == 0)
    def _():
        acc_sc[...] = part

    @pl.when(f > 0)
    def _():
        acc_sc[...] += part

    if n_exp > 1:
        @pl.when(f == last_f)
        def _():
            lane = lax.broadcasted_iota(jnp.int32, comb_sc.shape, 1)
            gate = jnp.sum(jnp.where(lane == e, comb_sc[...], 0.0), -1, keepdims=True)
            mix_sc[...] += gate * acc_sc[...]

    @pl.when((e == n_exp - 1) & (f == last_f))
    def _():
        y = mix_sc[...] if n_exp > 1 else acc_sc[...]
        z = DEEPNORM_ALPHA * x_ref[...] + y
        y_ref[...] = _layer_norm(z, g_ref[...], b_ref[...])


def ffn_norm(x, w_router, b_router, w_gate_up, w_down, g, b, tm, tf):
    n = x.shape[0]
    n_exp, d_ff = w_down.shape[0], w_down.shape[1]
    nf = d_ff // tf
    const = lambda *shape: pl.BlockSpec(shape, lambda i, e, f: (0,) * len(shape))
    mix_rows = tm if n_exp > 1 else 8
    return pl.pallas_call(
        functools.partial(_ffn_kernel, n_exp=n_exp),
        grid=(n // tm, n_exp, nf),
        in_specs=[
            pl.BlockSpec((tm, D_MODEL), lambda i, e, f: (i, 0)),
            const(2, D_MODEL, LANES), const(1, LANES),
            pl.BlockSpec((None, D_MODEL, tf), lambda i, e, f: (e, 0, f)),
            pl.BlockSpec((None, D_MODEL, tf), lambda i, e, f: (e, 0, nf + f)),
            pl.BlockSpec((None, tf, D_MODEL), lambda i, e, f: (e, f, 0)),
            const(1, D_MODEL), const(1, D_MODEL),
        ],
        out_specs=pl.BlockSpec((tm, D_MODEL), lambda i, e, f: (i, 0)),
        out_shape=jax.ShapeDtypeStruct((n, D_MODEL), F32),
        scratch_shapes=[pltpu.VMEM((tm, D_MODEL), BF16), pltpu.VMEM((tm, D_MODEL), F32),
                        pltpu.VMEM((mix_rows, D_MODEL), F32), pltpu.VMEM((mix_rows, LANES), F32)],
        compiler_params=_params(("arbitrary", "arbitrary", "arbitrary"), 48 << 20),
        name="ffn_norm",
    )(x, w_router, b_router, w_gate_up, w_gate_up, w_down, g, b)


def _dil_proj_kernel(x_ref, w_ref, tab_ref, h_ref):
    xb = x_ref[...].astype(BF16)
    gw = 3 * DIL_WIDTH
    for g in range(len(DIL_GROUPS)):
        hg = _dot(xb, w_ref[:, g * gw:(g + 1) * gw])
        for c in range(2 * DIL_WIDTH // LANES):
            sl = slice(c * LANES, (c + 1) * LANES)
            h_ref[:, g * gw + c * LANES:g * gw + (c + 1) * LANES] = _rope(hg[:, sl], tab_ref,
                                                                         DIL_HEAD_DIM // 2)
        h_ref[:, g * gw + 2 * DIL_WIDTH:(g + 1) * gw] = hg[:, 2 * DIL_WIDTH:]


def dil_project(x, w_in, tab, tab_block, tm):
    n = x.shape[0]
    width = w_in.shape[1]
    return pl.pallas_call(
        _dil_proj_kernel,
        grid=(n // tm,),
        in_specs=[
            pl.BlockSpec((tm, D_MODEL), lambda i: (i, 0)),
            pl.BlockSpec((D_MODEL, width), lambda i: (0, 0)),
            pl.BlockSpec((3, tm, LANES), lambda i: (0, tab_block(i), 0)),
        ],
        out_specs=pl.BlockSpec((tm, width), lambda i: (i, 0)),
        out_shape=jax.ShapeDtypeStruct((n, width), F32),
        compiler_params=_params(("arbitrary",), 48 << 20),
        name="dil_project",
    )(x, w_in, tab)


def _dil_prompt_kernel(*refs, seq):
    n_g = len(DIL_GROUPS)
    qkv = refs[:3 * n_g]
    o_ref = refs[3 * n_g]
    dq_sc, dk_sc, dv_sc, rm_sc, rl_sc, ra_sc = refs[3 * n_g + 1:3 * n_g + 7]
    stats = refs[3 * n_g + 7:]
    blk = DIL_BLOCK
    n_blk = seq // blk
    lane = lax.broadcasted_iota(jnp.int32, (blk, LANES), 1)
    head0 = lane < DIL_HEAD_DIM
    qi = lax.broadcasted_iota(jnp.int32, (2 * blk, blk), 0) % blk
    ki = lax.broadcasted_iota(jnp.int32, (2 * blk, blk), 1)
    scale = DIL_HEAD_DIM ** -0.5

    for g, (window, dil) in enumerate(DIL_GROUPS):
        q_ref, k_ref, v_ref = qkv[3 * g:3 * g + 3]
        m_ref, l_ref, a_ref = stats[3 * g:3 * g + 3]
        sub = seq // dil
        per_res = sub // blk
        if dil > 1:
            for r in range(dil):
                dq_sc[r * sub:(r + 1) * sub, :] = q_ref[pl.ds(r, sub, stride=dil), :]
                dk_sc[r * sub:(r + 1) * sub, :] = k_ref[pl.ds(r, sub, stride=dil), :]
                dv_sc[r * sub:(r + 1) * sub, :] = v_ref[pl.ds(r, sub, stride=dil), :]
            qs, ks, vs = dq_sc, dk_sc, dv_sc
            ms, ls, accs = rm_sc, rl_sc, ra_sc
        else:
            qs, ks, vs = q_ref, k_ref, v_ref
            ms, ls, accs = m_ref, l_ref, a_ref

        def block(j, carry, qs=qs, ks=ks, vs=vs, ms=ms, ls=ls, accs=accs, per_res=per_res):
            cur = pl.ds(pl.multiple_of(j * blk, blk), blk)
            q = qs[cur, :]
            q2 = jnp.concatenate([jnp.where(head0, q, 0.0), jnp.where(head0, 0.0, q)], 0).astype(BF16)
            k_cur = ks[cur, :].astype(BF16)
            v_cur = vs[cur, :].astype(BF16)
            s_cur = jnp.where(ki <= qi, _dot_nt(q2, k_cur) * scale, -jnp.inf)
            if per_res > 1:
                has_prev = (j % per_res) > 0
                prev = pl.ds(pl.multiple_of(jnp.maximum(j - 1, 0) * blk, blk), blk)
                k_prev = ks[prev, :].astype(BF16)
                v_prev = vs[prev, :].astype(BF16)
                s_prev = jnp.where((ki >= qi) & has_prev, _dot_nt(q2, k_prev) * scale, -jnp.inf)
                m = jnp.maximum(jnp.max(s_cur, -1, keepdims=True), jnp.max(s_prev, -1, keepdims=True))
                p_prev = jnp.exp(s_prev - m)
                p_cur = jnp.exp(s_cur - m)
                l = jnp.sum(p_cur, -1, keepdims=True) + jnp.sum(p_prev, -1, keepdims=True)
                acc = _dot(p_cur.astype(BF16), v_cur) + _dot(p_prev.astype(BF16), v_prev)
            else:
                m = jnp.max(s_cur, -1, keepdims=True)
                p_cur = jnp.exp(s_cur - m)
                l = jnp.sum(p_cur, -1, keepdims=True)
                acc = _dot(p_cur.astype(BF16), v_cur)
            ms[cur, :] = jnp.where(head0, m[:blk], m[blk:])
            ls[cur, :] = jnp.where(head0, l[:blk], l[blk:])
            accs[cur, :] = jnp.where(head0, acc[:blk], acc[blk:])
            return carry

        lax.fori_loop(0, n_blk, block, 0)
        if dil > 1:
            for r in range(dil):
                m_ref[pl.ds(r, sub, stride=dil), :] = rm_sc[r * sub:(r + 1) * sub, :]
                l_ref[pl.ds(r, sub, stride=dil), :] = rl_sc[r * sub:(r + 1) * sub, :]
                a_ref[pl.ds(r, sub, stride=dil), :] = ra_sc[r * sub:(r + 1) * sub, :]

    m_all = stats[0][...]
    for g in range(1, n_g):
        m_all = jnp.maximum(m_all, stats[3 * g][...])
    num = jnp.zeros(o_ref.shape, F32)
    den = jnp.zeros(o_ref.shape, F32)
    for g in range(n_g):
        wgt = jnp.exp(stats[3 * g][...] - m_all)
        num += wgt * stats[3 * g + 2][...]
        den += wgt * stats[3 * g + 1][...]
    o_ref[...] = (num / den).astype(o_ref.dtype)


def dil_prompt_attention(h, batch, seq):
    n = h.shape[0]
    pairs = DIL_WIDTH // LANES
    per_group = 3 * pairs
    in_specs = []
    for g in range(len(DIL_GROUPS)):
        for part in range(3):
            in_specs.append(pl.BlockSpec(
                (seq, LANES), lambda b, hp, g=g, part=part: (b, g * per_group + part * pairs + hp)))
    slab = pltpu.VMEM((seq, LANES), F32)
    return pl.pallas_call(
        functools.partial(_dil_prompt_kernel, seq=seq),
        grid=(batch, pairs),
        in_specs=in_specs,
        out_specs=pl.BlockSpec((seq, LANES), lambda b, hp: (b, hp)),
        out_shape=jax.ShapeDtypeStruct((n, DIL_WIDTH), BF16),
        scratch_shapes=[slab] * (6 + 3 * len(DIL_GROUPS)),
        compiler_params=_params(("arbitrary", "arbitrary"), 48 << 20),
        name="dil_prompt_attention",
    )(*([h] * (3 * len(DIL_GROUPS))))


def _dil_sample_kernel(h_ref, *refs, t_new):
    n_g = len(DIL_GROUPS)
    bufs = refs[:n_g]
    o_ref = refs[n_g]
    gw = 3 * DIL_WIDTH
    rows = t_new * DIL_HEADS
    row = lax.broadcasted_iota(jnp.int32, (rows, DIL_WIDTH), 0)
    col_head = lax.broadcasted_iota(jnp.int32, (rows, DIL_WIDTH), 1) // DIL_HEAD_DIM
    own = col_head == row % DIL_HEADS
    scale = DIL_HEAD_DIM ** -0.5
    scores, values = [], []
    for g, (window, dil) in enumerate(DIL_GROUPS):
        buf = bufs[g]
        n_buf = buf.shape[0]
        hq = h_ref[:, g * gw:g * gw + DIL_WIDTH]
        q = jnp.concatenate(
            [jnp.broadcast_to(hq[t:t + 1, :], (DIL_HEADS, DIL_WIDTH)) for t in range(t_new)], 0)
        q = jnp.where(own, q, 0.0).astype(BF16)
        k_old = buf[:, :DIL_WIDTH].astype(BF16)
        k_new = h_ref[:, g * gw + DIL_WIDTH:g * gw + 2 * DIL_WIDTH].astype(BF16)
        for keys, first in ((k_old, 0), (k_new, n_buf)):
            s = _dot_nt(q, keys) * scale
            pos = first + lax.broadcasted_iota(jnp.int32, s.shape, 1)
            back = n_buf + lax.broadcasted_iota(jnp.int32, s.shape, 0) // DIL_HEADS - pos
            ok = (back >= 0) & (back % dil == 0) & (back <= window)
            scores.append(jnp.where(ok, s, -jnp.inf))
        values.append(buf[:, DIL_WIDTH:].astype(BF16))
        values.append(h_ref[:, g * gw + 2 * DIL_WIDTH:(g + 1) * gw].astype(BF16))
    m = scores[0].max(-1, keepdims=True)
    for s in scores[1:]:
        m = jnp.maximum(m, s.max(-1, keepdims=True))
    den = jnp.zeros((rows, 1), F32)
    num = jnp.zeros((rows, DIL_WIDTH), F32)
    for s, v in zip(scores, values):
        p = jnp.exp(s - m)
        den += p.sum(-1, keepdims=True)
        num += _dot(p.astype(BF16), v)
    o = jnp.where(own, num / den, 0.0)
    o_ref[...] = jnp.concatenate(
        [jnp.sum(o[t * DIL_HEADS:(t + 1) * DIL_HEADS], 0, keepdims=True) for t in range(t_new)], 0)


def dil_sample_attention(h, caches, layer):
    n_batch, t_new, width = h.shape
    in_specs = [pl.BlockSpec((None, t_new, width), lambda b: (b, 0, 0))]
    for c in caches:
        in_specs.append(pl.BlockSpec((None, None, c.shape[2], c.shape[3]), lambda b: (layer, b, 0, 0)))
    return pl.pallas_call(
        functools.partial(_dil_sample_kernel, t_new=t_new),
        grid=(n_batch,),
        in_specs=in_specs,
        out_specs=pl.BlockSpec((None, t_new, DIL_WIDTH), lambda b: (b, 0, 0)),
        out_shape=jax.ShapeDtypeStruct((n_batch, t_new, DIL_WIDTH), F32),
        compiler_params=_params(("arbitrary",), 48 << 20),
        name="dil_sample_attention",
    )(h, *caches)


def _prep_mla(w_in, q_norm, w_uq, kv_norm, w_uk, w_uv, w_o):
    pad_in = 512 - w_in.shape[1]
    w_uq_h = w_uq.reshape(MLA_Q_LORA, MLA_HEADS, MLA_NOPE + MLA_ROPE)
    w_uq_h = jnp.pad(w_uq_h, ((0, 0), (0, 0), (0, LANES - MLA_NOPE - MLA_ROPE)))
    w_ukt = jnp.pad(w_uk.transpose(1, 2, 0), ((0, 0), (0, LANES - MLA_NOPE), (0, 0)))
    w_uv_h = w_uv.transpose(1, 0, 2)
    zero = jnp.zeros_like(w_uv_h[0::2])
    w_uv_pair = jnp.concatenate([jnp.concatenate([w_uv_h[0::2], zero], 2),
                                 jnp.concatenate([zero, w_uv_h[1::2]], 2)], 1)
    return {
        "w_in": jnp.pad(w_in, ((0, 0), (0, pad_in))).astype(BF16),
        "q_norm": q_norm.reshape(1, -1), "kv_norm": kv_norm.reshape(1, -1),
        "w_uq": w_uq_h.reshape(MLA_Q_LORA, MLA_HEADS * LANES).astype(BF16),
        "w_ukt": w_ukt.astype(BF16), "w_uv_pair": w_uv_pair.astype(BF16), "w_o": w_o.astype(BF16),
    }


def _prep_router(w_router, b_router):
    w = jnp.pad(w_router, ((0, 0), (0, LANES - w_router.shape[1])))
    hi = w.astype(BF16)
    lo = (w - hi.astype(F32)).astype(BF16)
    return jnp.stack([hi, lo]), jnp.pad(b_router, (0, LANES - b_router.shape[0])).reshape(1, LANES)


def kernel(x_prompt, x_sample, cache_mla_ckv, cache_mla_kpe, cache_dil_w128, cache_dil_w512,
           cache_dil_w2048, page_table, mla_w_in, mla_q_norm, mla_w_uq, mla_kv_norm, mla_w_uk,
           mla_w_uv, mla_w_o, dil_w_in, dil_w_o, ffn_w_gate_up, ffn_w_down, moe_w_router,
           moe_b_router, moe_w_gate_up, moe_w_down, ln_mix_g, ln_mix_b, ln_ffn_g, ln_ffn_b):
    batch, seq, _ = x_prompt.shape
    n_dec, t_new, _ = x_sample.shape
    past = page_table.shape[1] * PAGE_SIZE
    n_p, n_s = batch * seq, n_dec * t_new
    xp = x_prompt.reshape(n_p, D_MODEL)
    xs = x_sample.reshape(n_s, D_MODEL)

    pos = jnp.concatenate([jnp.arange(seq), past + jnp.arange(n_s) % t_new])
    tab_mla = rope_tables(pos, MLA_ROPE // 2)
    tab_dil = rope_tables(pos, DIL_HEAD_DIM // 2)
    tm_p, tm_d = 512, 256
    prompt_tab = lambda tm: (lambda i: i % (seq // tm))
    sample_tab = lambda tm: (lambda i: seq // tm + i)

    dil_caches = [c.reshape(c.shape[0], c.shape[1], c.shape[2], 2 * DIL_WIDTH)
                  for c in (cache_dil_w128, cache_dil_w512, cache_dil_w2048)]
    dummy_router = (jnp.zeros((2, D_MODEL, LANES), BF16), jnp.zeros((1, LANES), F32))
    row = lambda v: v.reshape(1, -1)

    ckv_p, kpe_p, ckv_s, kpe_s, kv_p, kv_s = [], [], [], [], [], []
    for i in range(DEPTH):
        li = i // 2
        g_mix, b_mix, g_ffn, b_ffn = row(ln_mix_g[i]), row(ln_mix_b[i]), row(ln_ffn_g[i]), row(ln_ffn_b[i])
        if i % 2 == 0:
            w = _prep_mla(mla_w_in[li], mla_q_norm[li], mla_w_uq[li], mla_kv_norm[li], mla_w_uk[li],
                          mla_w_uv[li], mla_w_o[li])
            c_p, r_p, kfull_p, q_p = mla_project(xp, w, tab_mla, prompt_tab(tm_p), BF16, tm_p)
            o_p = mla_prompt_attention(q_p, kfull_p, batch, seq)
            xp = mla_output(o_p, w, xp, g_mix, b_mix, tm_p)
            c_s, r_s, _, q_s = mla_project(xs, w, tab_mla, sample_tab(n_s), F32, n_s)
            o_s = mla_sample_attention(q_s, c_s, r_s, cache_mla_ckv, cache_mla_kpe, page_table, li, t_new)
            xs = mla_output(o_s, w, xs, g_mix, b_mix, n_s)
            ckv_p.append(c_p)
            kpe_p.append(r_p[:, :MLA_ROPE])
            ckv_s.append(c_s)
            kpe_s.append(r_s[:, :MLA_ROPE])
            wgu = ffn_w_gate_up[li:li + 1].astype(BF16)
            wd = ffn_w_down[li:li + 1].astype(BF16)
            xp = ffn_norm(xp, *dummy_router, wgu, wd, g_ffn, b_ffn, 1024, 256)
            xs = ffn_norm(xs, *dummy_router, wgu, wd, g_ffn, b_ffn, n_s, 256)
        else:
            w_in = dil_w_in[li].astype(BF16)
            w_o = dil_w_o[li].astype(BF16)
            h_p = dil_project(xp, w_in, tab_dil, prompt_tab(tm_d), tm_d)
            o_p = dil_prompt_attention(h_p, batch, seq)
            xp = proj_norm(o_p, w_o, xp, g_mix, b_mix, tm_p)
            tm_s = min(tm_d, n_s)
            h_s = dil_project(xs, w_in, tab_dil, sample_tab(tm_s), tm_s)
            o_s = dil_sample_attention(h_s.reshape(n_dec, t_new, -1), dil_caches, li)
            xs = proj_norm(o_s.reshape(n_s, DIL_WIDTH), w_o, xs, g_mix, b_mix, n_s)
            kv_p.append(h_p.reshape(batch, seq, len(DIL_GROUPS), 3, DIL_WIDTH)[:, :, :, 1:])
            kv_s.append(h_s.reshape(n_dec, t_new, len(DIL_GROUPS), 3, DIL_WIDTH)[:, :, :, 1:])
            router = _prep_router(moe_w_router[li], moe_b_router[li])
            wgu = moe_w_gate_up[li].astype(BF16)
            wd = moe_w_down[li].astype(BF16)
            xp = ffn_norm(xp, *router, wgu, wd, g_ffn, b_ffn, 1024, 512)
            xs = ffn_norm(xs, *router, wgu, wd, g_ffn, b_ffn, n_s, 512)

    outs = [xp.reshape(batch, seq, D_MODEL), xs.reshape(n_dec, t_new, D_MODEL),
            jnp.stack(ckv_p).reshape(-1, batch, seq, MLA_KV_LORA),
            jnp.stack(kpe_p).reshape(-1, batch, seq, MLA_ROPE),
            jnp.stack(ckv_s).reshape(-1, n_dec, t_new, MLA_KV_LORA),
            jnp.stack(kpe_s).reshape(-1, n_dec, t_new, MLA_ROPE)]
    state_shape = lambda b, rows: (-1, b, rows, 2, DIL_HEADS, DIL_HEAD_DIM)
    for g, (window, _) in enumerate(DIL_GROUPS):
        keep = min(window, seq)
        outs.append(jnp.stack([kv[:, seq - keep:, g] for kv in kv_p]).reshape(state_shape(batch, keep)))
    for g, cache in enumerate(dil_caches):
        new_rows = jnp.stack([kv[:, :, g].reshape(n_dec, t_new, 2 * DIL_WIDTH) for kv in kv_s])
        full = jnp.concatenate([cache, new_rows], axis=2)
        keep = min(DIL_GROUPS[g][0], full.shape[2])
        outs.append(full[:, :, full.shape[2] - keep:].reshape(state_shape(n_dec, keep)))
    return tuple(outs)
```

```python
import functools

import jax
import jax.numpy as jnp
from jax import lax
from jax.experimental import pallas as pl
from jax.experimental.pallas import tpu as pltpu

F32 = jnp.float32
BF16 = jnp.bfloat16
I32 = jnp.int32

D_MODEL = 1024
DEPTH = 4
PAGE_SIZE = 128
MLA_HEADS = 16
MLA_NOPE = 64
MLA_ROPE = 32
MLA_VDIM = 64
MLA_Q_LORA = 256
MLA_KV_LORA = 128
MLA_SCALE = (MLA_NOPE + MLA_ROPE) ** -0.5
Q_PRESCALE = MLA_SCALE * 1.4426950408889634
DIL_GROUPS = ((128, 1), (512, 4), (2048, 16))
DIL_HEADS = 8
DIL_HEAD_DIM = 64
DIL_BLOCK = 128
DIL_WIDTH = DIL_HEADS * DIL_HEAD_DIM
N_EXPERTS = 8
ROPE_THETA = 10000.0
LN_EPS = 1e-5
RMS_EPS = 1e-6
DEEPNORM_ALPHA = (2 * DEPTH) ** 0.25

LANES = 128
SUBLANES = 8
ROW_CHUNKS = D_MODEL // LANES
V7X_VMEM_BUDGET = 56 * 1024 * 1024

MOE_TILE = 256
MOE_CHUNK = 2560
GATHER_STRIDE = MOE_TILE + 1


def _params(semantics, vmem_bytes):
    return pltpu.CompilerParams(dimension_semantics=semantics,
                                vmem_limit_bytes=min(int(vmem_bytes), V7X_VMEM_BUDGET))


def _dot(a, b):
    return jnp.dot(a, b, preferred_element_type=F32)


def _dot_nt(a, b):
    return lax.dot_general(a, b, (((1,), (1,)), ((), ())), preferred_element_type=F32)


def _layer_norm(z, g, b):
    mu = jnp.mean(z, -1, keepdims=True)
    zc = z - mu
    var = jnp.mean(zc * zc, -1, keepdims=True)
    return zc * lax.rsqrt(var + LN_EPS) * g + b


def _rms_norm(h, g):
    return h * lax.rsqrt(jnp.mean(h * h, -1, keepdims=True) + RMS_EPS) * g


def _rope(x, tab_ref, half):
    n = x.shape[-1]
    return (x * tab_ref[0] + pltpu.roll(x, half, 1) * tab_ref[1]
            + pltpu.roll(x, n - half, 1) * tab_ref[2])


def _store_row_chunked(ref, y):
    rows = y.shape[0]
    for k in range(ROW_CHUNKS):
        ref[pl.ds(k, rows, stride=ROW_CHUNKS), :] = y[:, k * LANES:(k + 1) * LANES]


def _read_staging(tile_sc, rows):
    return jnp.concatenate(
        [tile_sc[k * GATHER_STRIDE:k * GATHER_STRIDE + rows, :] for k in range(ROW_CHUNKS)], -1)


def _rope_table_kernel(pos_ref, freq_ref, tab_ref, *, half):
    ang = pos_ref[...] * freq_ref[...]
    c = jnp.cos(ang)
    s = jnp.sin(ang)
    lane = lax.broadcasted_iota(I32, ang.shape, 1)
    upper = (lane % (2 * half)) >= half
    tab_ref[0] = c
    tab_ref[1] = jnp.where(upper, s, 0.0)
    tab_ref[2] = jnp.where(upper, 0.0, -s)


def rope_tables(pos, half):
    p = pos.shape[0]
    lane = jnp.arange(LANES)
    freq = (ROPE_THETA ** (-(lane % half).astype(F32) / half)).reshape(1, LANES)
    return pl.pallas_call(
        functools.partial(_rope_table_kernel, half=half),
        out_shape=jax.ShapeDtypeStruct((3, p, LANES), F32),
        name=f"rope_tables_{half}",
    )(pos.astype(F32).reshape(p, 1), freq)


def _mla_proj_kernel(x_ref, win_ref, qn_ref, kvn_ref, wuq_ref, wukt_ref, tab_ref,
                     ckv_ref, kpe_ref, kfull_ref, q_ref):
    xb = x_ref[...].astype(BF16)
    h = _dot(xb, win_ref[...])
    c_q = _rms_norm(h[:, :MLA_Q_LORA], qn_ref[...])
    c_kv = _rms_norm(h[:, MLA_Q_LORA:MLA_Q_LORA + MLA_KV_LORA], kvn_ref[...])
    kpe = _rope(h[:, MLA_Q_LORA + MLA_KV_LORA:], tab_ref, MLA_ROPE // 2)
    ckv_ref[...] = c_kv
    kpe_ref[...] = kpe
    kfull_ref[...] = jnp.concatenate([c_kv, kpe], -1).astype(BF16)
    q = _dot(c_q.astype(BF16), wuq_ref[...])
    lane = lax.broadcasted_iota(I32, (q.shape[0], LANES), 1)
    for hd in range(MLA_HEADS):
        qc = q[:, hd * LANES:(hd + 1) * LANES]
        q_lat = _dot(qc.astype(BF16), wukt_ref[hd])
        q_pe = _rope(pltpu.roll(qc, LANES - MLA_NOPE, 1), tab_ref, MLA_ROPE // 2)
        q_pe = jnp.where(lane < MLA_ROPE, q_pe, 0.0)
        q_ref[hd] = (jnp.concatenate([q_lat, q_pe], -1) * Q_PRESCALE).astype(q_ref.dtype)


def mla_project(x, row0, n, w, tab, tab_block, q_dtype, tm):
    blk0 = row0 // tm
    const = lambda *shape: pl.BlockSpec(shape, lambda i: (0,) * len(shape))
    return pl.pallas_call(
        _mla_proj_kernel,
        grid=(n // tm,),
        in_specs=[
            pl.BlockSpec((tm, D_MODEL), lambda i: (blk0 + i, 0)),
            const(D_MODEL, 512), const(1, MLA_Q_LORA), const(1, MLA_KV_LORA),
            const(MLA_Q_LORA, MLA_HEADS * LANES), const(MLA_HEADS, LANES, LANES),
            pl.BlockSpec((3, tm, LANES), lambda i: (0, tab_block(i), 0)),
        ],
        out_specs=[
            pl.BlockSpec((tm, LANES), lambda i: (i, 0)),
            pl.BlockSpec((tm, LANES), lambda i: (i, 0)),
            pl.BlockSpec((tm, 2 * LANES), lambda i: (i, 0)),
            pl.BlockSpec((MLA_HEADS, tm, 2 * LANES), lambda i: (0, i, 0)),
        ],
        out_shape=[
            jax.ShapeDtypeStruct((n, LANES), F32),
            jax.ShapeDtypeStruct((n, LANES), F32),
            jax.ShapeDtypeStruct((n, 2 * LANES), BF16),
            jax.ShapeDtypeStruct((MLA_HEADS, n, 2 * LANES), q_dtype),
        ],
        compiler_params=_params(("arbitrary",), 48 << 20),
        name="mla_project",
    )(x, w["w_in"], w["q_norm"], w["kv_norm"], w["w_uq"], w["w_ukt"], tab)


def _mla_prompt_attn_kernel(q_ref, k_ref, o_ref, m_sc, l_sc, acc_sc, *, tq, tk):
    iq = pl.program_id(1)
    rows = q_ref.shape[0] * tq
    q = q_ref[...].reshape(rows, q_ref.shape[2])
    m_sc[...] = jnp.full(m_sc.shape, -jnp.inf, F32)
    l_sc[...] = jnp.zeros(l_sc.shape, F32)
    acc_sc[...] = jnp.zeros(acc_sc.shape, F32)

    def step(kt, masked):
        k = k_ref[pl.ds(pl.multiple_of(kt * tk, tk), tk), :]
        s = _dot_nt(q, k)
        if masked:
            qpos = iq * tq + lax.broadcasted_iota(I32, s.shape, 0) % tq
            kpos = kt * tk + lax.broadcasted_iota(I32, s.shape, 1)
            s = jnp.where(kpos <= qpos, s, -jnp.inf)
        m_prev = m_sc[...]
        m_new = jnp.maximum(m_prev, jnp.max(s, -1, keepdims=True))
        alpha = jnp.exp2(m_prev - m_new)
        ps = [jnp.exp2(s[:, c * LANES:(c + 1) * LANES] - m_new) for c in range(tk // LANES)]
        l_sc[...] = alpha * l_sc[...] + sum(ps[1:], ps[0])
        acc_sc[...] = alpha * acc_sc[...] + _dot(jnp.concatenate(ps, -1).astype(BF16),
                                                 k[:, :MLA_KV_LORA])
        m_sc[...] = m_new

    n_full = (iq * tq + 1) // tk
    n_all = (iq * tq + tq - 1) // tk + 1
    lax.fori_loop(0, n_full, lambda kt, c: (step(kt, False), c)[1], 0)
    lax.fori_loop(n_full, n_all, lambda kt, c: (step(kt, True), c)[1], 0)
    o = acc_sc[...] / jnp.sum(l_sc[...], -1, keepdims=True)
    o_ref[...] = o.reshape(o_ref.shape).astype(o_ref.dtype)


def mla_prompt_attention(q, kfull, batch, seq, tq=128, tk=256):
    heads, n, dq = q.shape
    nq = seq // tq
    rows = heads * tq
    return pl.pallas_call(
        functools.partial(_mla_prompt_attn_kernel, tq=tq, tk=tk),
        grid=(batch, nq),
        in_specs=[
            pl.BlockSpec((heads, tq, dq), lambda b, i: (0, b * nq + i, 0)),
            pl.BlockSpec((seq, dq), lambda b, i: (b, 0)),
        ],
        out_specs=pl.BlockSpec((heads, tq, MLA_KV_LORA), lambda b, i: (0, b * nq + i, 0)),
        out_shape=jax.ShapeDtypeStruct((heads, n, MLA_KV_LORA), BF16),
        scratch_shapes=[pltpu.VMEM((rows, LANES), F32), pltpu.VMEM((rows, LANES), F32),
                        pltpu.VMEM((rows, MLA_KV_LORA), F32)],
        compiler_params=_params(("arbitrary", "arbitrary"), 40 << 20),
        name="mla_prompt_attention",
    )(q, kfull)


def _mla_sample_attn_kernel(pt_ref, q_ref, cnew_ref, rnew_ref, *rest, group, pages, t_new):
    n_pg = group * pages
    ckv_refs = rest[:n_pg]
    kpe_refs = rest[n_pg:2 * n_pg]
    o_ref, m_sc, l_sc, acc_sc = rest[2 * n_pg:]
    j = pl.program_id(1)
    rows = q_ref.shape[0] * q_ref.shape[1]
    q = q_ref[...].reshape(rows, q_ref.shape[2])
    q_lat = q[:, :MLA_KV_LORA].astype(BF16)
    q_pe = q[:, MLA_KV_LORA:MLA_KV_LORA + MLA_ROPE].astype(BF16)

    @pl.when(j == 0)
    def _():
        m_sc[...] = jnp.full(m_sc.shape, -jnp.inf, F32)
        l_sc[...] = jnp.zeros(l_sc.shape, F32)
        acc_sc[...] = jnp.zeros(acc_sc.shape, F32)

    def update(bb, s, v):
        m_prev = m_sc[bb]
        m_new = jnp.maximum(m_prev, jnp.max(s, -1, keepdims=True))
        alpha = jnp.exp2(m_prev - m_new)
        p = jnp.exp2(s - m_new[:, :1])
        l_sc[bb] = alpha * l_sc[bb] + jnp.sum(p, -1, keepdims=True)
        acc_sc[bb] = alpha * acc_sc[bb] + _dot(p.astype(BF16), v)
        m_sc[bb] = m_new

    for bb in range(group):
        ck = jnp.concatenate([ckv_refs[bb * pages + p][...] for p in range(pages)], 0).astype(BF16)
        kp_t = jnp.concatenate([kpe_refs[bb * pages + p][...] for p in range(pages)], 1).astype(BF16)
        update(bb, _dot_nt(q_lat, ck) + _dot(q_pe, kp_t), ck)

    @pl.when(j == pl.num_programs(1) - 1)
    def _():
        cn = cnew_ref[...].astype(BF16)
        rn = rnew_ref[...][:, :MLA_ROPE].astype(BF16)
        s_new = _dot_nt(q_lat, cn) + _dot_nt(q_pe, rn)
        tok_q = lax.broadcasted_iota(I32, s_new.shape, 0) % (group * t_new)
        tok_k = lax.broadcasted_iota(I32, s_new.shape, 1)
        out = jnp.zeros(acc_sc.shape[1:], F32)
        for bb in range(group):
            ok = (tok_k // t_new == bb) & (tok_k % t_new <= tok_q % t_new)
            update(bb, jnp.where(ok, s_new, -jnp.inf), cn)
            sel = lax.broadcasted_iota(I32, out.shape, 0) % (group * t_new) // t_new == bb
            out = jnp.where(sel, acc_sc[bb] / l_sc[bb], out)
        o_ref[...] = out.reshape(o_ref.shape)


def mla_sample_attention(q, ckv_new, kpe_new, ckv_pool, kpe_pool_t, page_table, layer, t_new,
                         group=2, pages=8):
    heads, n, dq = q.shape
    n_batch, n_pages = page_table.shape
    pages = min(pages, n_pages)
    gt = group * t_new
    rows = heads * gt

    def pool_spec(bb, p, shape):
        def index(g, j, pt):
            return (layer, pt[(g * group + bb) * n_pages + j * pages + p], 0, 0)
        return pl.BlockSpec((None, None) + shape, index)

    in_specs = [
        pl.BlockSpec((heads, gt, dq), lambda g, j, pt: (0, g, 0)),
        pl.BlockSpec((gt, LANES), lambda g, j, pt: (g, 0)),
        pl.BlockSpec((gt, LANES), lambda g, j, pt: (g, 0)),
    ]
    in_specs += [pool_spec(bb, p, (PAGE_SIZE, MLA_KV_LORA)) for bb in range(group) for p in range(pages)]
    in_specs += [pool_spec(bb, p, (MLA_ROPE, PAGE_SIZE)) for bb in range(group) for p in range(pages)]
    grid_spec = pltpu.PrefetchScalarGridSpec(
        num_scalar_prefetch=1,
        grid=(n_batch // group, n_pages // pages),
        in_specs=in_specs,
        out_specs=pl.BlockSpec((heads, gt, MLA_KV_LORA), lambda g, j, pt: (0, g, 0)),
        scratch_shapes=[pltpu.VMEM((group, rows, LANES), F32), pltpu.VMEM((group, rows, LANES), F32),
                        pltpu.VMEM((group, rows, MLA_KV_LORA), F32)],
    )
    n_pg = group * pages
    return pl.pallas_call(
        functools.partial(_mla_sample_attn_kernel, group=group, pages=pages, t_new=t_new),
        grid_spec=grid_spec,
        out_shape=jax.ShapeDtypeStruct((heads, n, MLA_KV_LORA), F32),
        compiler_params=_params(("arbitrary", "arbitrary"), 32 << 20),
        name="mla_sample_attention",
    )(page_table.reshape(-1), q, ckv_new, kpe_new, *([ckv_pool] * n_pg), *([kpe_pool_t] * n_pg))


def _mla_out_kernel(o_ref, wuv_ref, wo_ref, x_ref, g_ref, b_ref, y_ref):
    parts = []
    for p in range(MLA_HEADS // 2):
        a = jnp.concatenate([o_ref[2 * p], o_ref[2 * p + 1]], -1).astype(BF16)
        parts.append(_dot(a, wuv_ref[p]).astype(BF16))
    v = jnp.concatenate(parts, -1)
    z = DEEPNORM_ALPHA * x_ref[...] + _dot(v, wo_ref[...])
    y_ref[...] = _layer_norm(z, g_ref[...], b_ref[...])


def mla_output(o_lat, w, x, row0, g, b, tm):
    heads, n, c = o_lat.shape
    blk0 = row0 // tm
    const = lambda *shape: pl.BlockSpec(shape, lambda i: (0,) * len(shape))
    return pl.pallas_call(
        _mla_out_kernel,
        grid=(n // tm,),
        in_specs=[
            pl.BlockSpec((heads, tm, c), lambda i: (0, i, 0)),
            const(heads // 2, 2 * c, LANES), const(heads * MLA_VDIM, D_MODEL),
            pl.BlockSpec((tm, D_MODEL), lambda i: (blk0 + i, 0)),
            const(1, D_MODEL), const(1, D_MODEL),
        ],
        out_specs=pl.BlockSpec((tm, D_MODEL), lambda i: (i, 0)),
        out_shape=jax.ShapeDtypeStruct((n, D_MODEL), F32),
        compiler_params=_params(("arbitrary",), 40 << 20),
        name="mla_output",
    )(o_lat, w["w_uv_pair"], w["w_o"], x, g, b)


def _proj_norm_kernel(ap_ref, as_ref, wo_ref, xp_ref, xs_ref, g_ref, b_ref, y_ref, yc_ref, *, n_prompt_tiles):
    def run(a_ref, x_ref):
        z = DEEPNORM_ALPHA * x_ref[...] + _dot(a_ref[...].astype(BF16), wo_ref[...])
        y = _layer_norm(z, g_ref[...], b_ref[...])
        y_ref[...] = y
        _store_row_chunked(yc_ref, y)

    is_prompt = pl.program_id(0) < n_prompt_tiles
    pl.when(is_prompt)(lambda: run(ap_ref, xp_ref))
    pl.when(jnp.logical_not(is_prompt))(lambda: run(as_ref, xs_ref))


def proj_norm(a_p, a_s, w_o, xp, xs, g, b, tm):
    n_p, k = a_p.shape
    n_s = a_s.shape[0]
    tp, ts = n_p // tm, n_s // tm
    n = n_p + n_s
    const = lambda *shape: pl.BlockSpec(shape, lambda i: (0,) * len(shape))
    p_idx = lambda i: (jnp.minimum(i, tp - 1), 0)
    s_idx = lambda i: (jnp.maximum(i - tp, 0), 0)
    return pl.pallas_call(
        functools.partial(_proj_norm_kernel, n_prompt_tiles=tp),
        grid=(tp + ts,),
        in_specs=[
            pl.BlockSpec((tm, k), p_idx), pl.BlockSpec((tm, k), s_idx), const(k, D_MODEL),
            pl.BlockSpec((tm, D_MODEL), p_idx), pl.BlockSpec((tm, D_MODEL), s_idx),
            const(1, D_MODEL), const(1, D_MODEL),
        ],
        out_specs=[pl.BlockSpec((tm, D_MODEL), lambda i: (i, 0)),
                   pl.BlockSpec((tm * ROW_CHUNKS, LANES), lambda i: (i, 0))],
        out_shape=[jax.ShapeDtypeStruct((n, D_MODEL), F32),
                   jax.ShapeDtypeStruct((n * ROW_CHUNKS, LANES), F32)],
        compiler_params=_params(("arbitrary",), 40 << 20),
        name="proj_norm",
    )(a_p, a_s, w_o, xp, xs, g, b)


def _ffn_kernel(x_ref, wg_ref, wu_ref, wd_ref, g_ref, b_ref, y_ref, xb_sc, acc_sc):
    f = pl.program_id(1)

    @pl.when(f == 0)
    def _():
        xb_sc[...] = x_ref[...].astype(BF16)

    xb = xb_sc[...]
    hg = _dot(xb, wg_ref[...])
    hu = _dot(xb, wu_ref[...])
    act = (hg / (1.0 + jnp.exp(-hg)) * hu).astype(BF16)
    part = _dot(act, wd_ref[...])

    @pl.when(f == 0)
    def _():
        acc_sc[...] = part

    @pl.when(f > 0)
    def _():
        acc_sc[...] += part

    @pl.when(f == pl.num_programs(1) - 1)
    def _():
        z = DEEPNORM_ALPHA * x_ref[...] + acc_sc[...]
        y_ref[...] = _layer_norm(z, g_ref[...], b_ref[...])


def ffn_norm(x, w_gate_up, w_down, g, b, tm, tf):
    n = x.shape[0]
    d_ff = w_down.shape[0]
    nf = d_ff // tf
    const = lambda *shape: pl.BlockSpec(shape, lambda i, f: (0,) * len(shape))
    return pl.pallas_call(
        _ffn_kernel,
        grid=(n // tm, nf),
        in_specs=[
            pl.BlockSpec((tm, D_MODEL), lambda i, f: (i, 0)),
            pl.BlockSpec((D_MODEL, tf), lambda i, f: (0, f)),
            pl.BlockSpec((D_MODEL, tf), lambda i, f: (0, nf + f)),
            pl.BlockSpec((tf, D_MODEL), lambda i, f: (f, 0)),
            const(1, D_MODEL), const(1, D_MODEL),
        ],
        out_specs=pl.BlockSpec((tm, D_MODEL), lambda i, f: (i, 0)),
        out_shape=jax.ShapeDtypeStruct((n, D_MODEL), F32),
        scratch_shapes=[pltpu.VMEM((tm, D_MODEL), BF16), pltpu.VMEM((tm, D_MODEL), F32)],
        compiler_params=_params(("arbitrary", "arbitrary"), 48 << 20),
        name="ffn_norm",
    )(x, w_gate_up, w_gate_up, w_down, g, b)


def _moe_route_kernel(x_ref, wr_ref, br_ref, pos_ref, gate_ref, te_ref, *, n_exp):
    x = x_ref[...]
    c = x.shape[0]
    x1 = x.astype(BF16)
    x2 = (x - x1.astype(F32)).astype(BF16)
    logits = (_dot(x1, wr_ref[0]) + _dot(x1, wr_ref[1]) + _dot(x2, wr_ref[0])
              + _dot(x2, wr_ref[1]) + br_ref[...])
    lane = lax.broadcasted_iota(I32, logits.shape, 1)
    logits = jnp.where(lane < n_exp, logits, -jnp.inf)
    v1 = jnp.max(logits, -1, keepdims=True)
    i1 = jnp.min(jnp.where(logits == v1, lane, LANES), -1, keepdims=True)
    sel1 = lane == i1
    rest = jnp.where(sel1, -jnp.inf, logits)
    v2 = jnp.max(rest, -1, keepdims=True)
    i2 = jnp.min(jnp.where(rest == v2, lane, LANES), -1, keepdims=True)
    sel2 = lane == i2
    e2 = jnp.exp(v2 - v1)
    den = 1.0 + e2
    gate_ref[...] = jnp.where(lane == 0, 1.0 / den, jnp.where(lane == 1, e2 / den, 0.0))

    onehot = jnp.where(sel1 | sel2, 1.0, 0.0).astype(BF16)
    blk = MOE_TILE
    r_i = lax.broadcasted_iota(I32, (blk, blk), 0)
    c_i = lax.broadcasted_iota(I32, (blk, blk), 1)
    below = jnp.where(c_i < r_i, 1.0, 0.0).astype(BF16)
    carry = jnp.zeros((1, LANES), F32)
    ranks = []
    for t in range(c // blk):
        ob = onehot[t * blk:(t + 1) * blk]
        ranks.append(_dot(below, ob) + carry)
        carry = carry + jnp.sum(ob.astype(F32), 0, keepdims=True)
    rank = jnp.concatenate(ranks, 0)
    n_tiles = jnp.floor((carry + (blk - 1.0)) * (1.0 / blk))
    e_r = lax.broadcasted_iota(I32, (LANES, LANES), 0)
    e_c = lax.broadcasted_iota(I32, (LANES, LANES), 1)
    before = jnp.where(e_r < e_c, 1.0, 0.0).astype(BF16)
    first_tile = _dot(jnp.broadcast_to(n_tiles, (SUBLANES, LANES)).astype(BF16), before)[:1]
    slot = first_tile * float(blk) + rank
    pos1 = jnp.sum(jnp.where(sel1, slot, 0.0), -1, keepdims=True)
    pos2 = jnp.sum(jnp.where(sel2, slot, 0.0), -1, keepdims=True)
    pos_ref[...] = jnp.where(lane == 0, pos1, jnp.where(lane == 1, pos2, 0.0)).astype(I32)

    tile_id = lax.broadcasted_iota(I32, te_ref.shape, 0).astype(F32)
    lane_t = lax.broadcasted_iota(I32, te_ref.shape, 1)
    ended = (first_tile + n_tiles <= tile_id) & (lane_t < n_exp)
    te = jnp.sum(jnp.where(ended, 1.0, 0.0), -1, keepdims=True)
    te_ref[...] = jnp.broadcast_to(te, te_ref.shape).astype(I32)


def moe_route(x, w_router, b_router, n_exp, chunk, tiles_per_chunk):
    n = x.shape[0]
    n_chunks = n // chunk
    te_rows = -(-tiles_per_chunk // SUBLANES) * SUBLANES
    const = lambda *shape: pl.BlockSpec(shape, lambda c: (0,) * len(shape))
    return pl.pallas_call(
        functools.partial(_moe_route_kernel, n_exp=n_exp),
        grid=(n_chunks,),
        in_specs=[pl.BlockSpec((chunk, D_MODEL), lambda c: (c, 0)),
                  const(2, D_MODEL, LANES), const(1, LANES)],
        out_specs=[pl.BlockSpec((chunk, LANES), lambda c: (c, 0)),
                   pl.BlockSpec((chunk, LANES), lambda c: (c, 0)),
                   pl.BlockSpec((te_rows, LANES), lambda c: (c, 0))],
        out_shape=[jax.ShapeDtypeStruct((n, LANES), I32), jax.ShapeDtypeStruct((n, LANES), F32),
                   jax.ShapeDtypeStruct((n_chunks * te_rows, LANES), I32)],
        compiler_params=_params(("arbitrary",), 48 << 20),
        name="moe_route",
    )(x, w_router, b_router)


def _moe_dispatch_kernel(pos_ref, xc_ref, o_ref, src_sm, tile_sc, *, rows_per_chunk):
    j = pl.program_id(1)
    n_tok = pos_ref.shape[1]

    @pl.when(j == 0)
    def _():
        def clear(i, c):
            src_sm[i] = 0
            return c
        lax.fori_loop(0, rows_per_chunk, clear, 0)

        def invert(t, c):
            src_sm[pos_ref[0, t]] = t
            src_sm[pos_ref[1, t]] = t
            return c
        lax.fori_loop(0, n_tok, invert, 0)

    base = j * MOE_TILE
    for r in range(MOE_TILE):
        tok = src_sm[base + r]
        row = xc_ref[pl.ds(pl.multiple_of(tok * ROW_CHUNKS, ROW_CHUNKS), ROW_CHUNKS), :]
        tile_sc[pl.ds(r, ROW_CHUNKS, stride=GATHER_STRIDE), :] = row
    o_ref[...] = _read_staging(tile_sc, MOE_TILE).astype(BF16)


def moe_dispatch(pos_t, x_chunked, tiles_per_chunk):
    n_chunks, _, chunk = pos_t.shape
    rows_per_chunk = tiles_per_chunk * MOE_TILE
    return pl.pallas_call(
        functools.partial(_moe_dispatch_kernel, rows_per_chunk=rows_per_chunk),
        grid=(n_chunks, tiles_per_chunk),
        in_specs=[
            pl.BlockSpec((None, 2, chunk), lambda c, j: (c, 0, 0), memory_space=pltpu.SMEM),
            pl.BlockSpec((chunk * ROW_CHUNKS, LANES), lambda c, j: (c, 0)),
        ],
        out_specs=pl.BlockSpec((MOE_TILE, D_MODEL), lambda c, j: (c * tiles_per_chunk + j, 0)),
        out_shape=jax.ShapeDtypeStruct((n_chunks * rows_per_chunk, D_MODEL), BF16),
        scratch_shapes=[pltpu.SMEM((rows_per_chunk,), I32),
                        pltpu.VMEM((ROW_CHUNKS * GATHER_STRIDE, LANES), F32)],
        compiler_params=_params(("arbitrary", "arbitrary"), 40 << 20),
        name="moe_dispatch",
    )(pos_t, x_chunked)


def _moe_expert_kernel(order_ref, expert_ref, nact_ref, x_ref, wgu_ref, wd_ref, o_ref, *, tf):
    i = pl.program_id(0)
    d_ff = wd_ref.shape[0]

    @pl.when(i < nact_ref[0])
    def _():
        x = x_ref[...]
        acc = jnp.zeros((x.shape[0], D_MODEL), F32)
        for f in range(d_ff // tf):
            hg = _dot(x, wgu_ref[:, f * tf:(f + 1) * tf])
            hu = _dot(x, wgu_ref[:, d_ff + f * tf:d_ff + (f + 1) * tf])
            act = (hg / (1.0 + jnp.exp(-hg)) * hu).astype(BF16)
            acc = acc + _dot(act, wd_ref[f * tf:(f + 1) * tf, :])
        _store_row_chunked(o_ref, acc)

    @pl.when(i >= nact_ref[0])
    def _():
        o_ref[...] = jnp.zeros(o_ref.shape, F32)


def moe_experts(order, tile_expert, n_active, xs, w_gate_up, w_down, tf=512):
    n_tiles = order.shape[0]
    d_ff = w_down.shape[1]
    grid_spec = pltpu.PrefetchScalarGridSpec(
        num_scalar_prefetch=3,
        grid=(n_tiles,),
        in_specs=[
            pl.BlockSpec((MOE_TILE, D_MODEL), lambda i, o, e, na: (o[i], 0)),
            pl.BlockSpec((None, D_MODEL, 2 * d_ff), lambda i, o, e, na: (e[i], 0, 0),
                         pipeline_mode=pl.Buffered(1)),
            pl.BlockSpec((None, d_ff, D_MODEL), lambda i, o, e, na: (e[i], 0, 0),
                         pipeline_mode=pl.Buffered(1)),
        ],
        out_specs=pl.BlockSpec((MOE_TILE * ROW_CHUNKS, LANES), lambda i, o, e, na: (o[i], 0)),
    )
    return pl.pallas_call(
        functools.partial(_moe_expert_kernel, tf=tf),
        grid_spec=grid_spec,
        out_shape=jax.ShapeDtypeStruct((n_tiles * MOE_TILE * ROW_CHUNKS, LANES), F32),
        compiler_params=_params(("arbitrary",), 48 << 20),
        name="moe_experts",
    )(order, tile_expert, n_active, xs, w_gate_up, w_down)


def _moe_combine_kernel(pos_ref, gate_ref, ys_ref, x_ref, g_ref, b_ref, y_ref, tile_sc):
    j = pl.program_id(1)
    tm = x_ref.shape[0]
    for r in range(tm):
        t = j * tm + r
        r0 = ys_ref[pl.ds(pl.multiple_of(pos_ref[0, t] * ROW_CHUNKS, ROW_CHUNKS), ROW_CHUNKS), :]
        r1 = ys_ref[pl.ds(pl.multiple_of(pos_ref[1, t] * ROW_CHUNKS, ROW_CHUNKS), ROW_CHUNKS), :]
        tile_sc[pl.ds(r, ROW_CHUNKS, stride=GATHER_STRIDE), :] = gate_ref[0, t] * r0 + gate_ref[1, t] * r1
    z = DEEPNORM_ALPHA * x_ref[...] + _read_staging(tile_sc, tm)
    y_ref[...] = _layer_norm(z, g_ref[...], b_ref[...])


def moe_combine(pos_t, gate_t, ys_chunked, x, g, b, tiles_per_chunk):
    n = x.shape[0]
    n_chunks, _, chunk = pos_t.shape
    tm = MOE_TILE
    per_chunk = chunk // tm
    rows_per_chunk = tiles_per_chunk * MOE_TILE
    smem = lambda: pl.BlockSpec((None, 2, chunk), lambda c, j: (c, 0, 0), memory_space=pltpu.SMEM)
    const = lambda *shape: pl.BlockSpec(shape, lambda c, j: (0,) * len(shape))
    return pl.pallas_call(
        _moe_combine_kernel,
        grid=(n_chunks, per_chunk),
        in_specs=[
            smem(), smem(),
            pl.BlockSpec((rows_per_chunk * ROW_CHUNKS, LANES), lambda c, j: (c, 0),
                         pipeline_mode=pl.Buffered(1)),
            pl.BlockSpec((tm, D_MODEL), lambda c, j: (c * per_chunk + j, 0)),
            const(1, D_MODEL), const(1, D_MODEL),
        ],
        out_specs=pl.BlockSpec((tm, D_MODEL), lambda c, j: (c * per_chunk + j, 0)),
        out_shape=jax.ShapeDtypeStruct((n, D_MODEL), F32),
        scratch_shapes=[pltpu.VMEM((ROW_CHUNKS * GATHER_STRIDE, LANES), F32)],
        compiler_params=_params(("arbitrary", "arbitrary"), 48 << 20),
        name="moe_combine",
    )(pos_t, gate_t, ys_chunked, x, g, b)


def moe_norm(x, x_chunked, w_router, b_router, w_gate_up, w_down, g, b):
    n = x.shape[0]
    n_exp = w_down.shape[0]
    chunk = max(c for c in range(MOE_TILE, MOE_CHUNK + 1, MOE_TILE) if n % c == 0)
    n_chunks = n // chunk
    tiles_per_chunk = 2 * chunk // MOE_TILE + n_exp
    pos, gate, te = moe_route(x, w_router, b_router, n_exp, chunk, tiles_per_chunk)
    to_smem = lambda a: a[:, :2].reshape(n_chunks, chunk, 2).transpose(0, 2, 1)
    pos_t, gate_t = to_smem(pos), to_smem(gate)
    tile_e = te.reshape(n_chunks, -1, LANES)[:, :tiles_per_chunk, 0].reshape(-1)
    order = jnp.argsort(tile_e, stable=True).astype(I32)
    sorted_e = tile_e[order]
    n_active = jnp.sum(tile_e < n_exp).astype(I32)
    last_e = sorted_e[jnp.maximum(n_active - 1, 0)]
    sorted_e = jnp.where(sorted_e < n_exp, sorted_e, last_e).astype(I32)
    xs = moe_dispatch(pos_t, x_chunked, tiles_per_chunk)
    ys = moe_experts(order, sorted_e, n_active.reshape(1), xs, w_gate_up, w_down)
    return moe_combine(pos_t, gate_t, ys, x, g, b, tiles_per_chunk)


def _dil_proj_kernel(x_ref, w_ref, tab_ref, h_ref):
    xb = x_ref[...].astype(BF16)
    gw = 3 * DIL_WIDTH
    for g in range(len(DIL_GROUPS)):
        hg = _dot(xb, w_ref[:, g * gw:(g + 1) * gw])
        for c in range(2 * DIL_WIDTH // LANES):
            sl = slice(c * LANES, (c + 1) * LANES)
            h_ref[:, g * gw + c * LANES:g * gw + (c + 1) * LANES] = _rope(hg[:, sl], tab_ref,
                                                                         DIL_HEAD_DIM // 2)
        h_ref[:, g * gw + 2 * DIL_WIDTH:(g + 1) * gw] = hg[:, 2 * DIL_WIDTH:]


def dil_project(x, row0, n, w_in, tab, tab_block, tm):
    blk0 = row0 // tm
    width = w_in.shape[1]
    return pl.pallas_call(
        _dil_proj_kernel,
        grid=(n // tm,),
        in_specs=[
            pl.BlockSpec((tm, D_MODEL), lambda i: (blk0 + i, 0)),
            pl.BlockSpec((D_MODEL, width), lambda i: (0, 0)),
            pl.BlockSpec((3, tm, LANES), lambda i: (0, tab_block(i), 0)),
        ],
        out_specs=pl.BlockSpec((tm, width), lambda i: (i, 0)),
        out_shape=jax.ShapeDtypeStruct((n, width), F32),
        compiler_params=_params(("arbitrary",), 48 << 20),
        name="dil_project",
    )(x, w_in, tab)


def _dil_prompt_kernel(*refs, seq):
    n_g = len(DIL_GROUPS)
    qkv = refs[:3 * n_g]
    o_ref = refs[3 * n_g]
    dq_sc, dk_sc, dv_sc, rm_sc, rl_sc, ra_sc = refs[3 * n_g + 1:3 * n_g + 7]
    stats = refs[3 * n_g + 7:]
    blk = DIL_BLOCK
    n_blk = seq // blk
    lane = lax.broadcasted_iota(I32, (blk, LANES), 1)
    head0 = lane < DIL_HEAD_DIM
    qi = lax.broadcasted_iota(I32, (2 * blk, blk), 0) % blk
    ki = lax.broadcasted_iota(I32, (2 * blk, blk), 1)
    scale = DIL_HEAD_DIM ** -0.5

    for g, (window, dil) in enumerate(DIL_GROUPS):
        q_ref, k_ref, v_ref = qkv[3 * g:3 * g + 3]
        m_ref, l_ref, a_ref = stats[3 * g:3 * g + 3]
        sub = seq // dil
        per_res = sub // blk
        if dil > 1:
            for r in range(dil):
                dq_sc[r * sub:(r + 1) * sub, :] = q_ref[pl.ds(r, sub, stride=dil), :]
                dk_sc[r * sub:(r + 1) * sub, :] = k_ref[pl.ds(r, sub, stride=dil), :]
                dv_sc[r * sub:(r + 1) * sub, :] = v_ref[pl.ds(r, sub, stride=dil), :]
            qs, ks, vs = dq_sc, dk_sc, dv_sc
            ms, ls, accs = rm_sc, rl_sc, ra_sc
        else:
            qs, ks, vs = q_ref, k_ref, v_ref
            ms, ls, accs = m_ref, l_ref, a_ref

        def block(j, carry, qs=qs, ks=ks, vs=vs, ms=ms, ls=ls, accs=accs, per_res=per_res):
            cur = pl.ds(pl.multiple_of(j * blk, blk), blk)
            q = qs[cur, :]
            q2 = jnp.concatenate([jnp.where(head0, q, 0.0), jnp.where(head0, 0.0, q)], 0).astype(BF16)
            k_cur = ks[cur, :].astype(BF16)
            v_cur = vs[cur, :].astype(BF16)
            s_cur = jnp.where(ki <= qi, _dot_nt(q2, k_cur) * scale, -jnp.inf)
            if per_res > 1:
                has_prev = (j % per_res) > 0
                prev = pl.ds(pl.multiple_of(jnp.maximum(j - 1, 0) * blk, blk), blk)
                k_prev = ks[prev, :].astype(BF16)
                v_prev = vs[prev, :].astype(BF16)
                s_prev = jnp.where((ki >= qi) & has_prev, _dot_nt(q2, k_prev) * scale, -jnp.inf)
                m = jnp.maximum(jnp.max(s_cur, -1, keepdims=True), jnp.max(s_prev, -1, keepdims=True))
                p_prev = jnp.exp(s_prev - m)
                p_cur = jnp.exp(s_cur - m)
                l = jnp.sum(p_cur, -1, keepdims=True) + jnp.sum(p_prev, -1, keepdims=True)
                acc = _dot(p_cur.astype(BF16), v_cur) + _dot(p_prev.astype(BF16), v_prev)
            else:
                m = jnp.max(s_cur, -1, keepdims=True)
                p_cur = jnp.exp(s_cur - m)
                l = jnp.sum(p_cur, -1, keepdims=True)
                acc = _dot(p_cur.astype(BF16), v_cur)
            ms[cur, :] = jnp.where(head0, m[:blk], m[blk:])
            ls[cur, :] = jnp.where(head0, l[:blk], l[blk:])
            accs[cur, :] = jnp.where(head0, acc[:blk], acc[blk:])
            return carry

        lax.fori_loop(0, n_blk, block, 0)
        if dil > 1:
            for r in range(dil):
                m_ref[pl.ds(r, sub, stride=dil), :] = rm_sc[r * sub:(r + 1) * sub, :]
                l_ref[pl.ds(r, sub, stride=dil), :] = rl_sc[r * sub:(r + 1) * sub, :]
                a_ref[pl.ds(r, sub, stride=dil), :] = ra_sc[r * sub:(r + 1) * sub, :]

    m_all = stats[0][...]
    for g in range(1, n_g):
        m_all = jnp.maximum(m_all, stats[3 * g][...])
    num = jnp.zeros(o_ref.shape, F32)
    den = jnp.zeros(o_ref.shape, F32)
    for g in range(n_g):
        wgt = jnp.exp(stats[3 * g][...] - m_all)
        num += wgt * stats[3 * g + 2][...]
        den += wgt * stats[3 * g + 1][...]
    o_ref[...] = (num / den).astype(o_ref.dtype)


def dil_prompt_attention(h, batch, seq):
    n = h.shape[0]
    for window, dil in DIL_GROUPS:
        assert window // dil == DIL_BLOCK and seq % (dil * DIL_BLOCK) == 0
    pairs = DIL_WIDTH // LANES
    per_group = 3 * pairs
    in_specs = []
    for g in range(len(DIL_GROUPS)):
        for part in range(3):
            in_specs.append(pl.BlockSpec(
                (seq, LANES), lambda b, hp, g=g, part=part: (b, g * per_group + part * pairs + hp)))
    slab = pltpu.VMEM((seq, LANES), F32)
    return pl.pallas_call(
        functools.partial(_dil_prompt_kernel, seq=seq),
        grid=(batch, pairs),
        in_specs=in_specs,
        out_specs=pl.BlockSpec((seq, LANES), lambda b, hp: (b, hp)),
        out_shape=jax.ShapeDtypeStruct((n, DIL_WIDTH), BF16),
        scratch_shapes=[slab] * (6 + 3 * len(DIL_GROUPS)),
        compiler_params=_params(("arbitrary", "arbitrary"), 48 << 20),
        name="dil_prompt_attention",
    )(*([h] * (3 * len(DIL_GROUPS))))


def _dil_sample_kernel(h_ref, *refs, t_new):
    n_g = len(DIL_GROUPS)
    bufs = refs[:n_g]
    o_ref = refs[n_g]
    gw = 3 * DIL_WIDTH
    rows = t_new * DIL_HEADS
    row = lax.broadcasted_iota(I32, (rows, DIL_WIDTH), 0)
    col_head = lax.broadcasted_iota(I32, (rows, DIL_WIDTH), 1) // DIL_HEAD_DIM
    own = col_head == row % DIL_HEADS
    scale = DIL_HEAD_DIM ** -0.5
    scores, values = [], []
    for g, (window, dil) in enumerate(DIL_GROUPS):
        buf = bufs[g]
        n_buf = buf.shape[0]
        hq = h_ref[:, g * gw:g * gw + DIL_WIDTH]
        q = jnp.concatenate(
            [jnp.broadcast_to(hq[t:t + 1, :], (DIL_HEADS, DIL_WIDTH)) for t in range(t_new)], 0)
        q = jnp.where(own, q, 0.0).astype(BF16)
        k_old = buf[:, :DIL_WIDTH].astype(BF16)
        k_new = h_ref[:, g * gw + DIL_WIDTH:g * gw + 2 * DIL_WIDTH].astype(BF16)
        for keys, first in ((k_old, 0), (k_new, n_buf)):
            s = _dot_nt(q, keys) * scale
            pos = first + lax.broadcasted_iota(I32, s.shape, 1)
            back = n_buf + lax.broadcasted_iota(I32, s.shape, 0) // DIL_HEADS - pos
            ok = (back >= 0) & (back % dil == 0) & (back <= window)
            scores.append(jnp.where(ok, s, -jnp.inf))
        values.append(buf[:, DIL_WIDTH:].astype(BF16))
        values.append(h_ref[:, g * gw + 2 * DIL_WIDTH:(g + 1) * gw].astype(BF16))
    m = scores[0].max(-1, keepdims=True)
    for s in scores[1:]:
        m = jnp.maximum(m, s.max(-1, keepdims=True))
    den = jnp.zeros((rows, 1), F32)
    num = jnp.zeros((rows, DIL_WIDTH), F32)
    for s, v in zip(scores, values):
        p = jnp.exp(s - m)
        den += p.sum(-1, keepdims=True)
        num += _dot(p.astype(BF16), v)
    o = jnp.where(own, num / den, 0.0)
    o_ref[...] = jnp.concatenate(
        [jnp.sum(o[t * DIL_HEADS:(t + 1) * DIL_HEADS], 0, keepdims=True) for t in range(t_new)], 0)


def dil_sample_attention(h, caches, layer):
    n_batch, t_new, width = h.shape
    in_specs = [pl.BlockSpec((None, t_new, width), lambda b: (b, 0, 0))]
    for c in caches:
        in_specs.append(pl.BlockSpec((None, None, c.shape[2], c.shape[3]), lambda b: (layer, b, 0, 0)))
    return pl.pallas_call(
        functools.partial(_dil_sample_kernel, t_new=t_new),
        grid=(n_batch,),
        in_specs=in_specs,
        out_specs=pl.BlockSpec((None, t_new, DIL_WIDTH), lambda b: (b, 0, 0)),
        out_shape=jax.ShapeDtypeStruct((n_batch, t_new, DIL_WIDTH), F32),
        compiler_params=_params(("arbitrary",), 48 << 20),
        name="dil_sample_attention",
    )(h, *caches)


def _prep_mla(w_in, q_norm, w_uq, kv_norm, w_uk, w_uv, w_o):
    pad_in = 512 - w_in.shape[1]
    w_uq_h = w_uq.reshape(MLA_Q_LORA, MLA_HEADS, MLA_NOPE + MLA_ROPE)
    w_uq_h = jnp.pad(w_uq_h, ((0, 0), (0, 0), (0, LANES - MLA_NOPE - MLA_ROPE)))
    w_ukt = jnp.pad(w_uk.transpose(1, 2, 0), ((0, 0), (0, LANES - MLA_NOPE), (0, 0)))
    w_uv_h = w_uv.transpose(1, 0, 2)
    zero = jnp.zeros_like(w_uv_h[0::2])
    w_uv_pair = jnp.concatenate([jnp.concatenate([w_uv_h[0::2], zero], 2),
                                 jnp.concatenate([zero, w_uv_h[1::2]], 2)], 1)
    return {
        "w_in": jnp.pad(w_in, ((0, 0), (0, pad_in))).astype(BF16),
        "q_norm": q_norm.reshape(1, -1), "kv_norm": kv_norm.reshape(1, -1),
        "w_uq": w_uq_h.reshape(MLA_Q_LORA, MLA_HEADS * LANES).astype(BF16),
        "w_ukt": w_ukt.astype(BF16), "w_uv_pair": w_uv_pair.astype(BF16), "w_o": w_o.astype(BF16),
    }


def _prep_router(w_router, b_router):
    w = jnp.pad(w_router, ((0, 0), (0, LANES - w_router.shape[1])))
    hi = w.astype(BF16)
    lo = (w - hi.astype(F32)).astype(BF16)
    return jnp.stack([hi, lo]), jnp.pad(b_router, (0, LANES - b_router.shape[0])).reshape(1, LANES)


def kernel(x_prompt, x_sample, cache_mla_ckv, cache_mla_kpe, cache_dil_w128, cache_dil_w512,
           cache_dil_w2048, page_table, mla_w_in, mla_q_norm, mla_w_uq, mla_kv_norm, mla_w_uk,
           mla_w_uv, mla_w_o, dil_w_in, dil_w_o, ffn_w_gate_up, ffn_w_down, moe_w_router,
           moe_b_router, moe_w_gate_up, moe_w_down, ln_mix_g, ln_mix_b, ln_ffn_g, ln_ffn_b):
    batch, seq, _ = x_prompt.shape
    n_dec, t_new, _ = x_sample.shape
    past = page_table.shape[1] * PAGE_SIZE
    n_p, n_s = batch * seq, n_dec * t_new
    xp = (x_prompt.reshape(n_p, D_MODEL), 0)
    xs = (x_sample.reshape(n_s, D_MODEL), 0)

    pos = jnp.concatenate([jnp.arange(seq), past + jnp.arange(n_s) % t_new])
    tab_mla = rope_tables(pos, MLA_ROPE // 2)
    tab_dil = rope_tables(pos, DIL_HEAD_DIM // 2)
    tm_p, tm_d = 512, 256
    tm_s = min(tm_p, n_s)
    tm_ds = min(tm_d, n_s)
    prompt_tab = lambda tm: (lambda i: i % (seq // tm))
    sample_tab = lambda tm: (lambda i: seq // tm + i)

    kpe_pool_t = cache_mla_kpe.transpose(0, 1, 3, 2)
    dil_caches = [c.reshape(c.shape[0], c.shape[1], c.shape[2], 2 * DIL_WIDTH)
                  for c in (cache_dil_w128, cache_dil_w512, cache_dil_w2048)]
    row = lambda v: v.reshape(1, -1)

    ckv_p, kpe_p, ckv_s, kpe_s, kv_p, kv_s = [], [], [], [], [], []
    for i in range(DEPTH):
        li = i // 2
        g_mix, b_mix, g_ffn, b_ffn = row(ln_mix_g[i]), row(ln_mix_b[i]), row(ln_ffn_g[i]), row(ln_ffn_b[i])
        if i % 2 == 0:
            w = _prep_mla(mla_w_in[li], mla_q_norm[li], mla_w_uq[li], mla_kv_norm[li], mla_w_uk[li],
                          mla_w_uv[li], mla_w_o[li])
            c_p, r_p, kfull_p, q_p = mla_project(*xp, n_p, w, tab_mla, prompt_tab(tm_p), BF16, tm_p)
            o_p = mla_prompt_attention(q_p, kfull_p, batch, seq)
            xp = (mla_output(o_p, w, *xp, g_mix, b_mix, tm_p), 0)
            c_s, r_s, _, q_s = mla_project(*xs, n_s, w, tab_mla, sample_tab(tm_s), F32, tm_s)
            o_s = mla_sample_attention(q_s, c_s, r_s, cache_mla_ckv, kpe_pool_t, page_table, li, t_new)
            xs = (mla_output(o_s, w, *xs, g_mix, b_mix, tm_s), 0)
            ckv_p.append(c_p)
            kpe_p.append(r_p[:, :MLA_ROPE])
            ckv_s.append(c_s)
            kpe_s.append(r_s[:, :MLA_ROPE])
            wgu = ffn_w_gate_up[li].astype(BF16)
            wd = ffn_w_down[li].astype(BF16)
            xp = (ffn_norm(xp[0], wgu, wd, g_ffn, b_ffn, 1024, 256), 0)
            xs = (ffn_norm(xs[0], wgu, wd, g_ffn, b_ffn, tm_s, 256), 0)
        else:
            w_in = dil_w_in[li].astype(BF16)
            w_o = dil_w_o[li].astype(BF16)
            h_p = dil_project(*xp, n_p, w_in, tab_dil, prompt_tab(tm_d), tm_d)
            o_p = dil_prompt_attention(h_p, batch, seq)
            h_s = dil_project(*xs, n_s, w_in, tab_dil, sample_tab(tm_ds), tm_ds)
            o_s = dil_sample_attention(h_s.reshape(n_dec, t_new, -1), dil_caches, li)
            kv_p.append(h_p.reshape(batch, seq, len(DIL_GROUPS), 3, DIL_WIDTH)[:, :, :, 1:])
            kv_s.append(h_s.reshape(n_dec, t_new, len(DIL_GROUPS), 3, DIL_WIDTH)[:, :, :, 1:])
            xp_rows = lax.slice_in_dim(xp[0], xp[1], xp[1] + n_p) if xp[0].shape[0] != n_p else xp[0]
            xs_rows = lax.slice_in_dim(xs[0], xs[1], xs[1] + n_s) if xs[0].shape[0] != n_s else xs[0]
            x_all, x_chunked = proj_norm(o_p, o_s.reshape(n_s, DIL_WIDTH).astype(BF16), w_o,
                                         xp_rows, xs_rows, g_mix, b_mix, tm_s)
            router = _prep_router(moe_w_router[li], moe_b_router[li])
            y_all = moe_norm(x_all, x_chunked, *router, moe_w_gate_up[li].astype(BF16),
                             moe_w_down[li].astype(BF16), g_ffn, b_ffn)
            xp, xs = (y_all, 0), (y_all, n_p)

    xp_out = lax.slice_in_dim(xp[0], xp[1], xp[1] + n_p)
    xs_out = lax.slice_in_dim(xs[0], xs[1], xs[1] + n_s)
    outs = [xp_out.reshape(batch, seq, D_MODEL), xs_out.reshape(n_dec, t_new, D_MODEL),
            jnp.stack(ckv_p).reshape(-1, batch, seq, MLA_KV_LORA),
            jnp.stack(kpe_p).reshape(-1, batch, seq, MLA_ROPE),
            jnp.stack(ckv_s).reshape(-1, n_dec, t_new, MLA_KV_LORA),
            jnp.stack(kpe_s).reshape(-1, n_dec, t_new, MLA_ROPE)]
    state_shape = lambda b, rows: (-1, b, rows, 2, DIL_HEADS, DIL_HEAD_DIM)
    for g, (window, _) in enumerate(DIL_GROUPS):
        keep = min(window, seq)
        outs.append(jnp.stack([kv[:, seq - keep:, g] for kv in kv_p]).reshape(state_shape(batch, keep)))
    for g, cache in enumerate(dil_caches):
        new_rows = jnp.stack([kv[:, :, g].reshape(n_dec, t_new, 2 * DIL_WIDTH) for kv in kv_s])
        full = jnp.concatenate([cache, new_rows], axis=2)
        keep = min(DIL_GROUPS[g][0], full.shape[2])
        outs.append(full[:, :, full.shape[2] - keep:].reshape(state_shape(n_dec, keep)))
    return tuple(outs)
```

```python
import functools

import jax
import jax.numpy as jnp
from jax import lax
from jax.experimental import pallas as pl
from jax.experimental.pallas import tpu as pltpu

F32 = jnp.float32
BF16 = jnp.bfloat16
I32 = jnp.int32

D_MODEL = 1024
DEPTH = 4
PAGE_SIZE = 128
MLA_HEADS = 16
MLA_NOPE = 64
MLA_ROPE = 32
MLA_VDIM = 64
MLA_Q_LORA = 256
MLA_KV_LORA = 128
MLA_SCALE = (MLA_NOPE + MLA_ROPE) ** -0.5
Q_PRESCALE = MLA_SCALE * 1.4426950408889634
DIL_GROUPS = ((128, 1), (512, 4), (2048, 16))
DIL_HEADS = 8
DIL_HEAD_DIM = 64
DIL_BLOCK = 128
DIL_WIDTH = DIL_HEADS * DIL_HEAD_DIM
N_EXPERTS = 8
ROPE_THETA = 10000.0
LN_EPS = 1e-5
RMS_EPS = 1e-6
DEEPNORM_ALPHA = (2 * DEPTH) ** 0.25

LANES = 128
SUBLANES = 8
ROW_CHUNKS = D_MODEL // LANES
V7X_VMEM_BUDGET = 56 * 1024 * 1024

MOE_TILE = 256
MOE_CHUNK = 2560
GATHER_STRIDE = MOE_TILE + 1


def _params(semantics, vmem_bytes):
    return pltpu.CompilerParams(dimension_semantics=semantics,
                                vmem_limit_bytes=min(int(vmem_bytes), V7X_VMEM_BUDGET))


def _dot(a, b):
    return jnp.dot(a, b, preferred_element_type=F32)


def _dot_nt(a, b):
    return lax.dot_general(a, b, (((1,), (1,)), ((), ())), preferred_element_type=F32)


def _layer_norm(z, g, b):
    mu = jnp.mean(z, -1, keepdims=True)
    zc = z - mu
    var = jnp.mean(zc * zc, -1, keepdims=True)
    return zc * lax.rsqrt(var + LN_EPS) * g + b


def _rms_norm(h, g):
    return h * lax.rsqrt(jnp.mean(h * h, -1, keepdims=True) + RMS_EPS) * g


def _rope(x, tab_ref, half):
    n = x.shape[-1]
    return (x * tab_ref[0] + pltpu.roll(x, half, 1) * tab_ref[1]
            + pltpu.roll(x, n - half, 1) * tab_ref[2])


def _store_row_chunked(ref, y):
    rows = y.shape[0]
    for k in range(ROW_CHUNKS):
        ref[pl.ds(k, rows, stride=ROW_CHUNKS), :] = y[:, k * LANES:(k + 1) * LANES]


def _read_staging(tile_sc, rows):
    return jnp.concatenate(
        [tile_sc[k * GATHER_STRIDE:k * GATHER_STRIDE + rows, :] for k in range(ROW_CHUNKS)], -1)


def _rope_table_kernel(pos_ref, freq_ref, tab_ref, *, half):
    ang = pos_ref[...] * freq_ref[...]
    c = jnp.cos(ang)
    s = jnp.sin(ang)
    lane = lax.broadcasted_iota(I32, ang.shape, 1)
    upper = (lane % (2 * half)) >= half
    tab_ref[0] = c
    tab_ref[1] = jnp.where(upper, s, 0.0)
    tab_ref[2] = jnp.where(upper, 0.0, -s)


def rope_tables(pos, half):
    p = pos.shape[0]
    lane = jnp.arange(LANES)
    freq = (ROPE_THETA ** (-(lane % half).astype(F32) / half)).reshape(1, LANES)
    return pl.pallas_call(
        functools.partial(_rope_table_kernel, half=half),
        out_shape=jax.ShapeDtypeStruct((3, p, LANES), F32),
        name=f"rope_tables_{half}",
    )(pos.astype(F32).reshape(p, 1), freq)


def _mla_proj_kernel(x_ref, win_ref, qn_ref, kvn_ref, wuq_ref, wukt_ref, tab_ref,
                     ckv_ref, kpe_ref, kfull_ref, q_ref):
    xb = x_ref[...].astype(BF16)
    h = _dot(xb, win_ref[...])
    c_q = _rms_norm(h[:, :MLA_Q_LORA], qn_ref[...])
    c_kv = _rms_norm(h[:, MLA_Q_LORA:MLA_Q_LORA + MLA_KV_LORA], kvn_ref[...])
    kpe = _rope(h[:, MLA_Q_LORA + MLA_KV_LORA:], tab_ref, MLA_ROPE // 2)
    ckv_ref[...] = c_kv
    kpe_ref[...] = kpe
    kfull_ref[...] = jnp.concatenate([c_kv, kpe], -1).astype(BF16)
    q = _dot(c_q.astype(BF16), wuq_ref[...])
    lane = lax.broadcasted_iota(I32, (q.shape[0], LANES), 1)
    for hd in range(MLA_HEADS):
        qc = q[:, hd * LANES:(hd + 1) * LANES]
        q_lat = _dot(qc.astype(BF16), wukt_ref[hd])
        q_pe = _rope(pltpu.roll(qc, LANES - MLA_NOPE, 1), tab_ref, MLA_ROPE // 2)
        q_pe = jnp.where(lane < MLA_ROPE, q_pe, 0.0)
        q_ref[hd] = (jnp.concatenate([q_lat, q_pe], -1) * Q_PRESCALE).astype(q_ref.dtype)


def mla_project(x, row0, n, w, tab, tab_block, q_dtype, tm):
    blk0 = row0 // tm
    const = lambda *shape: pl.BlockSpec(shape, lambda i: (0,) * len(shape))
    return pl.pallas_call(
        _mla_proj_kernel,
        grid=(n // tm,),
        in_specs=[
            pl.BlockSpec((tm, D_MODEL), lambda i: (blk0 + i, 0)),
            const(D_MODEL, 512), const(1, MLA_Q_LORA), const(1, MLA_KV_LORA),
            const(MLA_Q_LORA, MLA_HEADS * LANES), const(MLA_HEADS, LANES, LANES),
            pl.BlockSpec((3, tm, LANES), lambda i: (0, tab_block(i), 0)),
        ],
        out_specs=[
            pl.BlockSpec((tm, LANES), lambda i: (i, 0)),
            pl.BlockSpec((tm, LANES), lambda i: (i, 0)),
            pl.BlockSpec((tm, 2 * LANES), lambda i: (i, 0)),
            pl.BlockSpec((MLA_HEADS, tm, 2 * LANES), lambda i: (0, i, 0)),
        ],
        out_shape=[
            jax.ShapeDtypeStruct((n, LANES), F32),
            jax.ShapeDtypeStruct((n, LANES), F32),
            jax.ShapeDtypeStruct((n, 2 * LANES), BF16),
            jax.ShapeDtypeStruct((MLA_HEADS, n, 2 * LANES), q_dtype),
        ],
        compiler_params=_params(("arbitrary",), 48 << 20),
        name="mla_project",
    )(x, w["w_in"], w["q_norm"], w["kv_norm"], w["w_uq"], w["w_ukt"], tab)


def _mla_prompt_attn_kernel(q_ref, k_ref, o_ref, m_sc, l_sc, acc_sc, *, tq, tk):
    iq = pl.program_id(1)
    rows = q_ref.shape[0] * tq
    q = q_ref[...].reshape(rows, q_ref.shape[2])
    m_sc[...] = jnp.full(m_sc.shape, -jnp.inf, F32)
    l_sc[...] = jnp.zeros(l_sc.shape, F32)
    acc_sc[...] = jnp.zeros(acc_sc.shape, F32)

    def step(kt, masked):
        k = k_ref[pl.ds(pl.multiple_of(kt * tk, tk), tk), :]
        s = _dot_nt(q, k)
        if masked:
            qpos = iq * tq + lax.broadcasted_iota(I32, s.shape, 0) % tq
            kpos = kt * tk + lax.broadcasted_iota(I32, s.shape, 1)
            s = jnp.where(kpos <= qpos, s, -jnp.inf)
        m_prev = m_sc[...]
        m_new = jnp.maximum(m_prev, jnp.max(s, -1, keepdims=True))
        alpha = jnp.exp2(m_prev - m_new)
        ps = [jnp.exp2(s[:, c * LANES:(c + 1) * LANES] - m_new) for c in range(tk // LANES)]
        l_sc[...] = alpha * l_sc[...] + sum(ps[1:], ps[0])
        acc_sc[...] = alpha * acc_sc[...] + _dot(jnp.concatenate(ps, -1).astype(BF16),
                                                 k[:, :MLA_KV_LORA])
        m_sc[...] = m_new

    n_full = (iq * tq + 1) // tk
    n_all = (iq * tq + tq - 1) // tk + 1
    lax.fori_loop(0, n_full, lambda kt, c: (step(kt, False), c)[1], 0)
    lax.fori_loop(n_full, n_all, lambda kt, c: (step(kt, True), c)[1], 0)
    o = acc_sc[...] / jnp.sum(l_sc[...], -1, keepdims=True)
    o_ref[...] = o.reshape(o_ref.shape).astype(o_ref.dtype)


def mla_prompt_attention(q, kfull, batch, seq, tq=128, tk=256):
    heads, n, dq = q.shape
    nq = seq // tq
    rows = heads * tq
    return pl.pallas_call(
        functools.partial(_mla_prompt_attn_kernel, tq=tq, tk=tk),
        grid=(batch, nq),
        in_specs=[
            pl.BlockSpec((heads, tq, dq), lambda b, i: (0, b * nq + i, 0)),
            pl.BlockSpec((seq, dq), lambda b, i: (b, 0)),
        ],
        out_specs=pl.BlockSpec((heads, tq, MLA_KV_LORA), lambda b, i: (0, b * nq + i, 0)),
        out_shape=jax.ShapeDtypeStruct((heads, n, MLA_KV_LORA), BF16),
        scratch_shapes=[pltpu.VMEM((rows, LANES), F32), pltpu.VMEM((rows, LANES), F32),
                        pltpu.VMEM((rows, MLA_KV_LORA), F32)],
        compiler_params=_params(("arbitrary", "arbitrary"), 40 << 20),
        name="mla_prompt_attention",
    )(q, kfull)


def _mla_sample_attn_kernel(pt_ref, q_ref, cnew_ref, rnew_ref, *rest, group, pages, t_new):
    n_pg = group * pages
    ckv_refs = rest[:n_pg]
    kpe_refs = rest[n_pg:2 * n_pg]
    o_ref, m_sc, l_sc, acc_sc = rest[2 * n_pg:]
    j = pl.program_id(1)
    rows = q_ref.shape[0] * q_ref.shape[1]
    q = q_ref[...].reshape(rows, q_ref.shape[2])
    q_lat = q[:, :MLA_KV_LORA].astype(BF16)
    q_pe = q[:, MLA_KV_LORA:MLA_KV_LORA + MLA_ROPE].astype(BF16)

    @pl.when(j == 0)
    def _():
        m_sc[...] = jnp.full(m_sc.shape, -jnp.inf, F32)
        l_sc[...] = jnp.zeros(l_sc.shape, F32)
        acc_sc[...] = jnp.zeros(acc_sc.shape, F32)

    def update(bb, s, v):
        m_prev = m_sc[bb]
        m_new = jnp.maximum(m_prev, jnp.max(s, -1, keepdims=True))
        alpha = jnp.exp2(m_prev - m_new)
        p = jnp.exp2(s - m_new[:, :1])
        l_sc[bb] = alpha * l_sc[bb] + jnp.sum(p, -1, keepdims=True)
        acc_sc[bb] = alpha * acc_sc[bb] + _dot(p.astype(BF16), v)
        m_sc[bb] = m_new

    for bb in range(group):
        ck = jnp.concatenate([ckv_refs[bb * pages + p][...] for p in range(pages)], 0).astype(BF16)
        kp_t = jnp.concatenate([kpe_refs[bb * pages + p][...] for p in range(pages)], 1).astype(BF16)
        update(bb, _dot_nt(q_lat, ck) + _dot(q_pe, kp_t), ck)

    @pl.when(j == pl.num_programs(1) - 1)
    def _():
        cn = cnew_ref[...].astype(BF16)
        rn = rnew_ref[...][:, :MLA_ROPE].astype(BF16)
        s_new = _dot_nt(q_lat, cn) + _dot_nt(q_pe, rn)
        tok_q = lax.broadcasted_iota(I32, s_new.shape, 0) % (group * t_new)
        tok_k = lax.broadcasted_iota(I32, s_new.shape, 1)
        out = jnp.zeros(acc_sc.shape[1:], F32)
        for bb in range(group):
            ok = (tok_k // t_new == bb) & (tok_k % t_new <= tok_q % t_new)
            update(bb, jnp.where(ok, s_new, -jnp.inf), cn)
            sel = lax.broadcasted_iota(I32, out.shape, 0) % (group * t_new) // t_new == bb
            out = jnp.where(sel, acc_sc[bb] / l_sc[bb], out)
        o_ref[...] = out.reshape(o_ref.shape)


def mla_sample_attention(q, ckv_new, kpe_new, ckv_pool, kpe_pool_t, page_table, layer, t_new,
                         group=2, pages=8):
    heads, n, dq = q.shape
    n_batch, n_pages = page_table.shape
    pages = min(pages, n_pages)
    gt = group * t_new
    rows = heads * gt

    def pool_spec(bb, p, shape):
        def index(g, j, pt):
            return (layer, pt[(g * group + bb) * n_pages + j * pages + p], 0, 0)
        return pl.BlockSpec((None, None) + shape, index)

    in_specs = [
        pl.BlockSpec((heads, gt, dq), lambda g, j, pt: (0, g, 0)),
        pl.BlockSpec((gt, LANES), lambda g, j, pt: (g, 0)),
        pl.BlockSpec((gt, LANES), lambda g, j, pt: (g, 0)),
    ]
    in_specs += [pool_spec(bb, p, (PAGE_SIZE, MLA_KV_LORA)) for bb in range(group) for p in range(pages)]
    in_specs += [pool_spec(bb, p, (MLA_ROPE, PAGE_SIZE)) for bb in range(group) for p in range(pages)]
    grid_spec = pltpu.PrefetchScalarGridSpec(
        num_scalar_prefetch=1,
        grid=(n_batch // group, n_pages // pages),
        in_specs=in_specs,
        out_specs=pl.BlockSpec((heads, gt, MLA_KV_LORA), lambda g, j, pt: (0, g, 0)),
        scratch_shapes=[pltpu.VMEM((group, rows, LANES), F32), pltpu.VMEM((group, rows, LANES), F32),
                        pltpu.VMEM((group, rows, MLA_KV_LORA), F32)],
    )
    n_pg = group * pages
    return pl.pallas_call(
        functools.partial(_mla_sample_attn_kernel, group=group, pages=pages, t_new=t_new),
        grid_spec=grid_spec,
        out_shape=jax.ShapeDtypeStruct((heads, n, MLA_KV_LORA), F32),
        compiler_params=_params(("arbitrary", "arbitrary"), 32 << 20),
        name="mla_sample_attention",
    )(page_table.reshape(-1), q, ckv_new, kpe_new, *([ckv_pool] * n_pg), *([kpe_pool_t] * n_pg))


def _mla_out_kernel(o_ref, wuv_ref, wo_ref, x_ref, g_ref, b_ref, y_ref):
    parts = []
    for p in range(MLA_HEADS // 2):
        a = jnp.concatenate([o_ref[2 * p], o_ref[2 * p + 1]], -1).astype(BF16)
        parts.append(_dot(a, wuv_ref[p]).astype(BF16))
    v = jnp.concatenate(parts, -1)
    z = DEEPNORM_ALPHA * x_ref[...] + _dot(v, wo_ref[...])
    y_ref[...] = _layer_norm(z, g_ref[...], b_ref[...])


def mla_output(o_lat, w, x, row0, g, b, tm):
    heads, n, c = o_lat.shape
    blk0 = row0 // tm
    const = lambda *shape: pl.BlockSpec(shape, lambda i: (0,) * len(shape))
    return pl.pallas_call(
        _mla_out_kernel,
        grid=(n // tm,),
        in_specs=[
            pl.BlockSpec((heads, tm, c), lambda i: (0, i, 0)),
            const(heads // 2, 2 * c, LANES), const(heads * MLA_VDIM, D_MODEL),
            pl.BlockSpec((tm, D_MODEL), lambda i: (blk0 + i, 0)),
            const(1, D_MODEL), const(1, D_MODEL),
        ],
        out_specs=pl.BlockSpec((tm, D_MODEL), lambda i: (i, 0)),
        out_shape=jax.ShapeDtypeStruct((n, D_MODEL), F32),
        compiler_params=_params(("arbitrary",), 40 << 20),
        name="mla_output",
    )(o_lat, w["w_uv_pair"], w["w_o"], x, g, b)


def _proj_norm_kernel(ap_ref, as_ref, wo_ref, xp_ref, xs_ref, g_ref, b_ref, y_ref, yc_ref, *, n_prompt_tiles):
    def run(a_ref, x_ref):
        z = DEEPNORM_ALPHA * x_ref[...] + _dot(a_ref[...].astype(BF16), wo_ref[...])
        y = _layer_norm(z, g_ref[...], b_ref[...])
        y_ref[...] = y
        _store_row_chunked(yc_ref, y)

    is_prompt = pl.program_id(0) < n_prompt_tiles
    pl.when(is_prompt)(lambda: run(ap_ref, xp_ref))
    pl.when(jnp.logical_not(is_prompt))(lambda: run(as_ref, xs_ref))


def proj_norm(a_p, a_s, w_o, xp, xs, g, b, tm):
    n_p, k = a_p.shape
    n_s = a_s.shape[0]
    tp, ts = n_p // tm, n_s // tm
    n = n_p + n_s
    const = lambda *shape: pl.BlockSpec(shape, lambda i: (0,) * len(shape))
    p_idx = lambda i: (jnp.minimum(i, tp - 1), 0)
    s_idx = lambda i: (jnp.maximum(i - tp, 0), 0)
    return pl.pallas_call(
        functools.partial(_proj_norm_kernel, n_prompt_tiles=tp),
        grid=(tp + ts,),
        in_specs=[
            pl.BlockSpec((tm, k), p_idx), pl.BlockSpec((tm, k), s_idx), const(k, D_MODEL),
            pl.BlockSpec((tm, D_MODEL), p_idx), pl.BlockSpec((tm, D_MODEL), s_idx),
            const(1, D_MODEL), const(1, D_MODEL),
        ],
        out_specs=[pl.BlockSpec((tm, D_MODEL), lambda i: (i, 0)),
                   pl.BlockSpec((tm * ROW_CHUNKS, LANES), lambda i: (i, 0))],
        out_shape=[jax.ShapeDtypeStruct((n, D_MODEL), F32),
                   jax.ShapeDtypeStruct((n * ROW_CHUNKS, LANES), F32)],
        compiler_params=_params(("arbitrary",), 40 << 20),
        name="proj_norm",
    )(a_p, a_s, w_o, xp, xs, g, b)


def _ffn_kernel(x_ref, wg_ref, wu_ref, wd_ref, g_ref, b_ref, y_ref, xb_sc, acc_sc):
    f = pl.program_id(1)

    @pl.when(f == 0)
    def _():
        xb_sc[...] = x_ref[...].astype(BF16)

    xb = xb_sc[...]
    hg = _dot(xb, wg_ref[...])
    hu = _dot(xb, wu_ref[...])
    act = (hg / (1.0 + jnp.exp(-hg)) * hu).astype(BF16)
    part = _dot(act, wd_ref[...])

    @pl.when(f == 0)
    def _():
        acc_sc[...] = part

    @pl.when(f > 0)
    def _():
        acc_sc[...] += part

    @pl.when(f == pl.num_programs(1) - 1)
    def _():
        z = DEEPNORM_ALPHA * x_ref[...] + acc_sc[...]
        y_ref[...] = _layer_norm(z, g_ref[...], b_ref[...])


def ffn_norm(x, w_gate_up, w_down, g, b, tm, tf):
    n = x.shape[0]
    d_ff = w_down.shape[0]
    nf = d_ff // tf
    const = lambda *shape: pl.BlockSpec(shape, lambda i, f: (0,) * len(shape))
    return pl.pallas_call(
        _ffn_kernel,
        grid=(n // tm, nf),
        in_specs=[
            pl.BlockSpec((tm, D_MODEL), lambda i, f: (i, 0)),
            pl.BlockSpec((D_MODEL, tf), lambda i, f: (0, f)),
            pl.BlockSpec((D_MODEL, tf), lambda i, f: (0, nf + f)),
            pl.BlockSpec((tf, D_MODEL), lambda i, f: (f, 0)),
            const(1, D_MODEL), const(1, D_MODEL),
        ],
        out_specs=pl.BlockSpec((tm, D_MODEL), lambda i, f: (i, 0)),
        out_shape=jax.ShapeDtypeStruct((n, D_MODEL), F32),
        scratch_shapes=[pltpu.VMEM((tm, D_MODEL), BF16), pltpu.VMEM((tm, D_MODEL), F32)],
        compiler_params=_params(("arbitrary", "arbitrary"), 48 << 20),
        name="ffn_norm",
    )(x, w_gate_up, w_gate_up, w_down, g, b)


def _moe_route_kernel(x_ref, wr_ref, br_ref, pos_ref, gate_ref, te_ref, *, n_exp):
    x = x_ref[...]
    c = x.shape[0]
    x1 = x.astype(BF16)
    x2 = (x - x1.astype(F32)).astype(BF16)
    logits = (_dot(x1, wr_ref[0]) + _dot(x1, wr_ref[1]) + _dot(x2, wr_ref[0])
              + _dot(x2, wr_ref[1]) + br_ref[...])
    lane = lax.broadcasted_iota(I32, logits.shape, 1)
    logits = jnp.where(lane < n_exp, logits, -jnp.inf)
    v1 = jnp.max(logits, -1, keepdims=True)
    i1 = jnp.min(jnp.where(logits == v1, lane, LANES), -1, keepdims=True)
    sel1 = lane == i1
    rest = jnp.where(sel1, -jnp.inf, logits)
    v2 = jnp.max(rest, -1, keepdims=True)
    i2 = jnp.min(jnp.where(rest == v2, lane, LANES), -1, keepdims=True)
    sel2 = lane == i2
    e2 = jnp.exp(v2 - v1)
    den = 1.0 + e2
    gate_ref[...] = jnp.where(lane == 0, 1.0 / den, jnp.where(lane == 1, e2 / den, 0.0))

    onehot = jnp.where(sel1 | sel2, 1.0, 0.0).astype(BF16)
    blk = MOE_TILE
    r_i = lax.broadcasted_iota(I32, (blk, blk), 0)
    c_i = lax.broadcasted_iota(I32, (blk, blk), 1)
    below = jnp.where(c_i < r_i, 1.0, 0.0).astype(BF16)
    carry = jnp.zeros((1, LANES), F32)
    ranks = []
    for t in range(c // blk):
        ob = onehot[t * blk:(t + 1) * blk]
        ranks.append(_dot(below, ob) + carry)
        carry = carry + jnp.sum(ob.astype(F32), 0, keepdims=True)
    rank = jnp.concatenate(ranks, 0)
    n_tiles = jnp.floor((carry + (blk - 1.0)) * (1.0 / blk))
    e_r = lax.broadcasted_iota(I32, (LANES, LANES), 0)
    e_c = lax.broadcasted_iota(I32, (LANES, LANES), 1)
    before = jnp.where(e_r < e_c, 1.0, 0.0).astype(BF16)
    first_tile = _dot(jnp.broadcast_to(n_tiles, (SUBLANES, LANES)).astype(BF16), before)[:1]
    slot = first_tile * float(blk) + rank
    pos1 = jnp.sum(jnp.where(sel1, slot, 0.0), -1, keepdims=True)
    pos2 = jnp.sum(jnp.where(sel2, slot, 0.0), -1, keepdims=True)
    pos_ref[...] = jnp.where(lane == 0, pos1, jnp.where(lane == 1, pos2, 0.0)).astype(I32)

    tile_id = lax.broadcasted_iota(I32, te_ref.shape, 0).astype(F32)
    lane_t = lax.broadcasted_iota(I32, te_ref.shape, 1)
    ended = (first_tile + n_tiles <= tile_id) & (lane_t < n_exp)
    te = jnp.sum(jnp.where(ended, 1.0, 0.0), -1, keepdims=True)
    te_ref[...] = jnp.broadcast_to(te, te_ref.shape).astype(I32)


def moe_route(x, w_router, b_router, n_exp, chunk, tiles_per_chunk):
    n = x.shape[0]
    n_chunks = n // chunk
    te_rows = -(-tiles_per_chunk // SUBLANES) * SUBLANES
    const = lambda *shape: pl.BlockSpec(shape, lambda c: (0,) * len(shape))
    return pl.pallas_call(
        functools.partial(_moe_route_kernel, n_exp=n_exp),
        grid=(n_chunks,),
        in_specs=[pl.BlockSpec((chunk, D_MODEL), lambda c: (c, 0)),
                  const(2, D_MODEL, LANES), const(1, LANES)],
        out_specs=[pl.BlockSpec((chunk, LANES), lambda c: (c, 0)),
                   pl.BlockSpec((chunk, LANES), lambda c: (c, 0)),
                   pl.BlockSpec((te_rows, LANES), lambda c: (c, 0))],
        out_shape=[jax.ShapeDtypeStruct((n, LANES), I32), jax.ShapeDtypeStruct((n, LANES), F32),
                   jax.ShapeDtypeStruct((n_chunks * te_rows, LANES), I32)],
        compiler_params=_params(("arbitrary",), 48 << 20),
        name="moe_route",
    )(x, w_router, b_router)


def _moe_dispatch_kernel(pos_ref, xc_ref, o_ref, src_sm, tile_sc, *, rows_per_chunk):
    j = pl.program_id(1)
    n_tok = pos_ref.shape[1]

    @pl.when((j == 0) & (pl.program_id(0) == 0))
    def _():
        def clear(i, c):
            src_sm[i] = 0
            return c
        lax.fori_loop(0, rows_per_chunk, clear, 0, unroll=8)

    @pl.when(j == 0)
    def _():
        def invert(t, c):
            src_sm[pos_ref[0, t]] = t
            src_sm[pos_ref[1, t]] = t
            return c
        lax.fori_loop(0, n_tok, invert, 0, unroll=8)

    base = j * MOE_TILE
    for r in range(MOE_TILE):
        tok = src_sm[base + r]
        row = xc_ref[pl.ds(pl.multiple_of(tok * ROW_CHUNKS, ROW_CHUNKS), ROW_CHUNKS), :]
        tile_sc[pl.ds(r, ROW_CHUNKS, stride=GATHER_STRIDE), :] = row
    o_ref[...] = _read_staging(tile_sc, MOE_TILE).astype(BF16)


def moe_dispatch(pos_t, x_chunked, tiles_per_chunk):
    n_chunks, _, chunk = pos_t.shape
    rows_per_chunk = tiles_per_chunk * MOE_TILE
    return pl.pallas_call(
        functools.partial(_moe_dispatch_kernel, rows_per_chunk=rows_per_chunk),
        grid=(n_chunks, tiles_per_chunk),
        in_specs=[
            pl.BlockSpec((None, 2, chunk), lambda c, j: (c, 0, 0), memory_space=pltpu.SMEM),
            pl.BlockSpec((chunk * ROW_CHUNKS, LANES), lambda c, j: (c, 0)),
        ],
        out_specs=pl.BlockSpec((MOE_TILE, D_MODEL), lambda c, j: (c * tiles_per_chunk + j, 0)),
        out_shape=jax.ShapeDtypeStruct((n_chunks * rows_per_chunk, D_MODEL), BF16),
        scratch_shapes=[pltpu.SMEM((rows_per_chunk,), I32),
                        pltpu.VMEM((ROW_CHUNKS * GATHER_STRIDE, LANES), F32)],
        compiler_params=_params(("arbitrary", "arbitrary"), 40 << 20),
        name="moe_dispatch",
    )(pos_t, x_chunked)


def _moe_expert_kernel(order_ref, expert_ref, nact_ref, x_ref, wgu_ref, wd_ref, o_ref, *, tf):
    i = pl.program_id(0)
    d_ff = wd_ref.shape[0]

    @pl.when(i < nact_ref[0])
    def _():
        x = x_ref[...]
        acc = jnp.zeros((x.shape[0], D_MODEL), F32)
        for f in range(d_ff // tf):
            hg = _dot(x, wgu_ref[:, f * tf:(f + 1) * tf])
            hu = _dot(x, wgu_ref[:, d_ff + f * tf:d_ff + (f + 1) * tf])
            act = (hg / (1.0 + jnp.exp(-hg)) * hu).astype(BF16)
            acc = acc + _dot(act, wd_ref[f * tf:(f + 1) * tf, :])
        _store_row_chunked(o_ref, acc)

    @pl.when(i >= nact_ref[0])
    def _():
        o_ref[...] = jnp.zeros(o_ref.shape, F32)


def moe_experts(order, tile_expert, n_active, xs, w_gate_up, w_down, tf=512):
    n_tiles = order.shape[0]
    d_ff = w_down.shape[1]
    grid_spec = pltpu.PrefetchScalarGridSpec(
        num_scalar_prefetch=3,
        grid=(n_tiles,),
        in_specs=[
            pl.BlockSpec((MOE_TILE, D_MODEL), lambda i, o, e, na: (o[i], 0)),
            pl.BlockSpec((None, D_MODEL, 2 * d_ff), lambda i, o, e, na: (e[i], 0, 0),
                         pipeline_mode=pl.Buffered(1)),
            pl.BlockSpec((None, d_ff, D_MODEL), lambda i, o, e, na: (e[i], 0, 0),
                         pipeline_mode=pl.Buffered(1)),
        ],
        out_specs=pl.BlockSpec((MOE_TILE * ROW_CHUNKS, LANES), lambda i, o, e, na: (o[i], 0)),
    )
    return pl.pallas_call(
        functools.partial(_moe_expert_kernel, tf=tf),
        grid_spec=grid_spec,
        out_shape=jax.ShapeDtypeStruct((n_tiles * MOE_TILE * ROW_CHUNKS, LANES), F32),
        compiler_params=_params(("arbitrary",), 48 << 20),
        name="moe_experts",
    )(order, tile_expert, n_active, xs, w_gate_up, w_down)


def _moe_combine_kernel(pos_ref, gate_ref, ys_ref, x_ref, g_ref, b_ref, y_ref, tile_sc):
    j = pl.program_id(1)
    tm = x_ref.shape[0]
    for r in range(tm):
        t = j * tm + r
        r0 = ys_ref[pl.ds(pl.multiple_of(pos_ref[0, t] * ROW_CHUNKS, ROW_CHUNKS), ROW_CHUNKS), :]
        r1 = ys_ref[pl.ds(pl.multiple_of(pos_ref[1, t] * ROW_CHUNKS, ROW_CHUNKS), ROW_CHUNKS), :]
        tile_sc[pl.ds(r, ROW_CHUNKS, stride=GATHER_STRIDE), :] = gate_ref[0, t] * r0 + gate_ref[1, t] * r1
    z = DEEPNORM_ALPHA * x_ref[...] + _read_staging(tile_sc, tm)
    y_ref[...] = _layer_norm(z, g_ref[...], b_ref[...])


def moe_combine(pos_t, gate_t, ys_chunked, x, g, b, tiles_per_chunk):
    n = x.shape[0]
    n_chunks, _, chunk = pos_t.shape
    tm = MOE_TILE
    per_chunk = chunk // tm
    rows_per_chunk = tiles_per_chunk * MOE_TILE
    smem = lambda: pl.BlockSpec((None, 2, chunk), lambda c, j: (c, 0, 0), memory_space=pltpu.SMEM)
    const = lambda *shape: pl.BlockSpec(shape, lambda c, j: (0,) * len(shape))
    return pl.pallas_call(
        _moe_combine_kernel,
        grid=(n_chunks, per_chunk),
        in_specs=[
            smem(), smem(),
            pl.BlockSpec((rows_per_chunk * ROW_CHUNKS, LANES), lambda c, j: (c, 0),
                         pipeline_mode=pl.Buffered(1)),
            pl.BlockSpec((tm, D_MODEL), lambda c, j: (c * per_chunk + j, 0)),
            const(1, D_MODEL), const(1, D_MODEL),
        ],
        out_specs=pl.BlockSpec((tm, D_MODEL), lambda c, j: (c * per_chunk + j, 0)),
        out_shape=jax.ShapeDtypeStruct((n, D_MODEL), F32),
        scratch_shapes=[pltpu.VMEM((ROW_CHUNKS * GATHER_STRIDE, LANES), F32)],
        compiler_params=_params(("arbitrary", "arbitrary"), 48 << 20),
        name="moe_combine",
    )(pos_t, gate_t, ys_chunked, x, g, b)


def moe_norm(x, x_chunked, w_router, b_router, w_gate_up, w_down, g, b):
    n = x.shape[0]
    n_exp = w_down.shape[0]
    chunk = max(c for c in range(MOE_TILE, MOE_CHUNK + 1, MOE_TILE) if n % c == 0)
    n_chunks = n // chunk
    tiles_per_chunk = 2 * chunk // MOE_TILE + n_exp
    pos, gate, te = moe_route(x, w_router, b_router, n_exp, chunk, tiles_per_chunk)
    to_smem = lambda a: a[:, :2].reshape(n_chunks, chunk, 2).transpose(0, 2, 1)
    pos_t, gate_t = to_smem(pos), to_smem(gate)
    tile_e = te.reshape(n_chunks, -1, LANES)[:, :tiles_per_chunk, 0].reshape(-1)
    order = jnp.argsort(tile_e, stable=True).astype(I32)
    sorted_e = tile_e[order]
    n_active = jnp.sum(tile_e < n_exp).astype(I32)
    last_e = sorted_e[jnp.maximum(n_active - 1, 0)]
    sorted_e = jnp.where(sorted_e < n_exp, sorted_e, last_e).astype(I32)
    xs = moe_dispatch(pos_t, x_chunked, tiles_per_chunk)
    ys = moe_experts(order, sorted_e, n_active.reshape(1), xs, w_gate_up, w_down)
    return moe_combine(pos_t, gate_t, ys, x, g, b, tiles_per_chunk)


def _dil_proj_kernel(x_ref, w_ref, tab_ref, h_ref, *state_refs):
    xb = x_ref[...].astype(BF16)
    tm = xb.shape[0]
    gw = 3 * DIL_WIDTH
    for g in range(len(DIL_GROUPS)):
        hg = _dot(xb, w_ref[:, g * gw:(g + 1) * gw])
        roped = []
        for c in range(2 * DIL_WIDTH // LANES):
            roped.append(_rope(hg[:, c * LANES:(c + 1) * LANES], tab_ref, DIL_HEAD_DIM // 2))
            h_ref[:, g * gw + c * LANES:g * gw + (c + 1) * LANES] = roped[-1]
        h_ref[:, g * gw + 2 * DIL_WIDTH:(g + 1) * gw] = hg[:, 2 * DIL_WIDTH:]
        if state_refs:
            kv_t = jnp.transpose(jnp.concatenate(roped[DIL_WIDTH // LANES:] + [hg[:, 2 * DIL_WIDTH:]], -1))
            state_refs[g][...] = kv_t[:, tm - state_refs[g].shape[1]:]


def dil_project(x, row0, n, w_in, tab, tab_block, tm, state_seq=None):
    blk0 = row0 // tm
    width = w_in.shape[1]
    out_specs = [pl.BlockSpec((tm, width), lambda i: (i, 0))]
    out_shape = [jax.ShapeDtypeStruct((n, width), F32)]
    if state_seq is not None:
        per_batch = state_seq // tm
        for window, _ in DIL_GROUPS:
            keep = min(window, state_seq)
            wblk = min(keep, tm)
            first = (state_seq - keep) // tm
            out_specs.append(pl.BlockSpec(
                (None, 2 * DIL_WIDTH, wblk),
                lambda i, first=first: (i // per_batch, 0, jnp.maximum(i % per_batch - first, 0))))
            out_shape.append(jax.ShapeDtypeStruct((n // state_seq, 2 * DIL_WIDTH, keep), F32))
    return pl.pallas_call(
        _dil_proj_kernel,
        grid=(n // tm,),
        in_specs=[
            pl.BlockSpec((tm, D_MODEL), lambda i: (blk0 + i, 0)),
            pl.BlockSpec((D_MODEL, width), lambda i: (0, 0)),
            pl.BlockSpec((3, tm, LANES), lambda i: (0, tab_block(i), 0)),
        ],
        out_specs=out_specs,
        out_shape=out_shape,
        compiler_params=_params(("arbitrary",), 52 << 20),
        name="dil_project",
    )(x, w_in, tab)


def _dil_prompt_kernel(*refs, seq):
    n_g = len(DIL_GROUPS)
    qkv = refs[:3 * n_g]
    o_ref = refs[3 * n_g]
    dq_sc, dk_sc, dv_sc, rm_sc, rl_sc, ra_sc = refs[3 * n_g + 1:3 * n_g + 7]
    stats = refs[3 * n_g + 7:]
    blk = DIL_BLOCK
    n_blk = seq // blk
    lane = lax.broadcasted_iota(I32, (blk, LANES), 1)
    head0 = lane < DIL_HEAD_DIM
    qi = lax.broadcasted_iota(I32, (2 * blk, blk), 0) % blk
    ki = lax.broadcasted_iota(I32, (2 * blk, blk), 1)
    scale = DIL_HEAD_DIM ** -0.5

    for g, (window, dil) in enumerate(DIL_GROUPS):
        q_ref, k_ref, v_ref = qkv[3 * g:3 * g + 3]
        m_ref, l_ref, a_ref = stats[3 * g:3 * g + 3]
        sub = seq // dil
        per_res = sub // blk
        if dil > 1:
            for r in range(dil):
                dq_sc[r * sub:(r + 1) * sub, :] = q_ref[pl.ds(r, sub, stride=dil), :]
                dk_sc[r * sub:(r + 1) * sub, :] = k_ref[pl.ds(r, sub, stride=dil), :]
                dv_sc[r * sub:(r + 1) * sub, :] = v_ref[pl.ds(r, sub, stride=dil), :]
            qs, ks, vs = dq_sc, dk_sc, dv_sc
            ms, ls, accs = rm_sc, rl_sc, ra_sc
        else:
            qs, ks, vs = q_ref, k_ref, v_ref
            ms, ls, accs = m_ref, l_ref, a_ref

        def block(j, carry, qs=qs, ks=ks, vs=vs, ms=ms, ls=ls, accs=accs, per_res=per_res):
            cur = pl.ds(pl.multiple_of(j * blk, blk), blk)
            q = qs[cur, :]
            q2 = jnp.concatenate([jnp.where(head0, q, 0.0), jnp.where(head0, 0.0, q)], 0).astype(BF16)
            k_cur = ks[cur, :].astype(BF16)
            v_cur = vs[cur, :].astype(BF16)
            s_cur = jnp.where(ki <= qi, _dot_nt(q2, k_cur) * scale, -jnp.inf)
            if per_res > 1:
                has_prev = (j % per_res) > 0
                prev = pl.ds(pl.multiple_of(jnp.maximum(j - 1, 0) * blk, blk), blk)
                k_prev = ks[prev, :].astype(BF16)
                v_prev = vs[prev, :].astype(BF16)
                s_prev = jnp.where((ki >= qi) & has_prev, _dot_nt(q2, k_prev) * scale, -jnp.inf)
                m = jnp.maximum(jnp.max(s_cur, -1, keepdims=True), jnp.max(s_prev, -1, keepdims=True))
                p_prev = jnp.exp(s_prev - m)
                p_cur = jnp.exp(s_cur - m)
                l = jnp.sum(p_cur, -1, keepdims=True) + jnp.sum(p_prev, -1, keepdims=True)
                acc = _dot(p_cur.astype(BF16), v_cur) + _dot(p_prev.astype(BF16), v_prev)
            else:
                m = jnp.max(s_cur, -1, keepdims=True)
                p_cur = jnp.exp(s_cur - m)
                l = jnp.sum(p_cur, -1, keepdims=True)
                acc = _dot(p_cur.astype(BF16), v_cur)
            ms[cur, :] = jnp.where(head0, m[:blk], m[blk:])
            ls[cur, :] = jnp.where(head0, l[:blk], l[blk:])
            accs[cur, :] = jnp.where(head0, acc[:blk], acc[blk:])
            return carry

        lax.fori_loop(0, n_blk, block, 0, unroll=4)
        if dil > 1:
            for r in range(dil):
                m_ref[pl.ds(r, sub, stride=dil), :] = rm_sc[r * sub:(r + 1) * sub, :]
                l_ref[pl.ds(r, sub, stride=dil), :] = rl_sc[r * sub:(r + 1) * sub, :]
                a_ref[pl.ds(r, sub, stride=dil), :] = ra_sc[r * sub:(r + 1) * sub, :]

    m_all = stats[0][...]
    for g in range(1, n_g):
        m_all = jnp.maximum(m_all, stats[3 * g][...])
    num = jnp.zeros(o_ref.shape, F32)
    den = jnp.zeros(o_ref.shape, F32)
    for g in range(n_g):
        wgt = jnp.exp(stats[3 * g][...] - m_all)
        num += wgt * stats[3 * g + 2][...]
        den += wgt * stats[3 * g + 1][...]
    o_ref[...] = (num / den).astype(o_ref.dtype)


def dil_prompt_attention(h, batch, seq):
    n = h.shape[0]
    for window, dil in DIL_GROUPS:
        assert window // dil == DIL_BLOCK and seq % (dil * DIL_BLOCK) == 0
    pairs = DIL_WIDTH // LANES
    per_group = 3 * pairs
    in_specs = []
    for g in range(len(DIL_GROUPS)):
        for part in range(3):
            in_specs.append(pl.BlockSpec(
                (seq, LANES), lambda b, hp, g=g, part=part: (b, g * per_group + part * pairs + hp)))
    slab = pltpu.VMEM((seq, LANES), F32)
    return pl.pallas_call(
        functools.partial(_dil_prompt_kernel, seq=seq),
        grid=(batch, pairs),
        in_specs=in_specs,
        out_specs=pl.BlockSpec((seq, LANES), lambda b, hp: (b, hp)),
        out_shape=jax.ShapeDtypeStruct((n, DIL_WIDTH), BF16),
        scratch_shapes=[slab] * (6 + 3 * len(DIL_GROUPS)),
        compiler_params=_params(("arbitrary", "arbitrary"), 48 << 20),
        name="dil_prompt_attention",
    )(*([h] * (3 * len(DIL_GROUPS))))


def _dil_sample_kernel(h_ref, *refs, t_new):
    n_g = len(DIL_GROUPS)
    bufs = refs[:n_g]
    o_ref = refs[n_g]
    gw = 3 * DIL_WIDTH
    rows = t_new * DIL_HEADS
    row = lax.broadcasted_iota(I32, (rows, DIL_WIDTH), 0)
    col_head = lax.broadcasted_iota(I32, (rows, DIL_WIDTH), 1) // DIL_HEAD_DIM
    own = col_head == row % DIL_HEADS
    scale = DIL_HEAD_DIM ** -0.5
    scores, values = [], []
    for g, (window, dil) in enumerate(DIL_GROUPS):
        buf = bufs[g]
        n_buf = buf.shape[1]
        hq = h_ref[:, g * gw:g * gw + DIL_WIDTH]
        q = jnp.concatenate(
            [jnp.broadcast_to(hq[t:t + 1, :], (DIL_HEADS, DIL_WIDTH)) for t in range(t_new)], 0)
        q = jnp.where(own, q, 0.0).astype(BF16)
        k_old_t = buf[:DIL_WIDTH, :].astype(BF16)
        k_new = h_ref[:, g * gw + DIL_WIDTH:g * gw + 2 * DIL_WIDTH].astype(BF16)
        for s, first in ((_dot(q, k_old_t), 0), (_dot_nt(q, k_new), n_buf)):
            pos = first + lax.broadcasted_iota(I32, s.shape, 1)
            back = n_buf + lax.broadcasted_iota(I32, s.shape, 0) // DIL_HEADS - pos
            ok = (back >= 0) & (back % dil == 0) & (back <= window)
            scores.append(jnp.where(ok, s * scale, -jnp.inf))
        values.append((buf[DIL_WIDTH:, :].astype(BF16), True))
        values.append((h_ref[:, g * gw + 2 * DIL_WIDTH:(g + 1) * gw].astype(BF16), False))
    m = scores[0].max(-1, keepdims=True)
    for s in scores[1:]:
        m = jnp.maximum(m, s.max(-1, keepdims=True))
    den = jnp.zeros((rows, 1), F32)
    num = jnp.zeros((rows, DIL_WIDTH), F32)
    for s, (v, transposed) in zip(scores, values):
        p = jnp.exp(s - m)
        den += p.sum(-1, keepdims=True)
        num += _dot_nt(p.astype(BF16), v) if transposed else _dot(p.astype(BF16), v)
    o = jnp.where(own, num / den, 0.0)
    o_ref[...] = jnp.concatenate(
        [jnp.sum(o[t * DIL_HEADS:(t + 1) * DIL_HEADS], 0, keepdims=True) for t in range(t_new)], 0)


def dil_sample_attention(h, caches, layer):
    n_batch, t_new, width = h.shape
    in_specs = [pl.BlockSpec((None, t_new, width), lambda b: (b, 0, 0))]
    for c in caches:
        in_specs.append(pl.BlockSpec((None, None, c.shape[2], c.shape[3]), lambda b: (layer, b, 0, 0)))
    return pl.pallas_call(
        functools.partial(_dil_sample_kernel, t_new=t_new),
        grid=(n_batch,),
        in_specs=in_specs,
        out_specs=pl.BlockSpec((None, t_new, DIL_WIDTH), lambda b: (b, 0, 0)),
        out_shape=jax.ShapeDtypeStruct((n_batch, t_new, DIL_WIDTH), F32),
        compiler_params=_params(("arbitrary",), 48 << 20),
        name="dil_sample_attention",
    )(h, *caches)


def _prep_mla(w_in, q_norm, w_uq, kv_norm, w_uk, w_uv, w_o):
    pad_in = 512 - w_in.shape[1]
    w_uq_h = w_uq.reshape(MLA_Q_LORA, MLA_HEADS, MLA_NOPE + MLA_ROPE)
    w_uq_h = jnp.pad(w_uq_h, ((0, 0), (0, 0), (0, LANES - MLA_NOPE - MLA_ROPE)))
    w_ukt = jnp.pad(w_uk.transpose(1, 2, 0), ((0, 0), (0, LANES - MLA_NOPE), (0, 0)))
    w_uv_h = w_uv.transpose(1, 0, 2)
    zero = jnp.zeros_like(w_uv_h[0::2])
    w_uv_pair = jnp.concatenate([jnp.concatenate([w_uv_h[0::2], zero], 2),
                                 jnp.concatenate([zero, w_uv_h[1::2]], 2)], 1)
    return {
        "w_in": jnp.pad(w_in, ((0, 0), (0, pad_in))).astype(BF16),
        "q_norm": q_norm.reshape(1, -1), "kv_norm": kv_norm.reshape(1, -1),
        "w_uq": w_uq_h.reshape(MLA_Q_LORA, MLA_HEADS * LANES).astype(BF16),
        "w_ukt": w_ukt.astype(BF16), "w_uv_pair": w_uv_pair.astype(BF16), "w_o": w_o.astype(BF16),
    }


def _prep_router(w_router, b_router):
    w = jnp.pad(w_router, ((0, 0), (0, LANES - w_router.shape[1])))
    hi = w.astype(BF16)
    lo = (w - hi.astype(F32)).astype(BF16)
    return jnp.stack([hi, lo]), jnp.pad(b_router, (0, LANES - b_router.shape[0])).reshape(1, LANES)


def kernel(x_prompt, x_sample, cache_mla_ckv, cache_mla_kpe, cache_dil_w128, cache_dil_w512,
           cache_dil_w2048, page_table, mla_w_in, mla_q_norm, mla_w_uq, mla_kv_norm, mla_w_uk,
           mla_w_uv, mla_w_o, dil_w_in, dil_w_o, ffn_w_gate_up, ffn_w_down, moe_w_router,
           moe_b_router, moe_w_gate_up, moe_w_down, ln_mix_g, ln_mix_b, ln_ffn_g, ln_ffn_b):
    batch, seq, _ = x_prompt.shape
    n_dec, t_new, _ = x_sample.shape
    past = page_table.shape[1] * PAGE_SIZE
    n_p, n_s = batch * seq, n_dec * t_new
    xp = (x_prompt.reshape(n_p, D_MODEL), 0)
    xs = (x_sample.reshape(n_s, D_MODEL), 0)

    pos = jnp.concatenate([jnp.arange(seq), past + jnp.arange(n_s) % t_new])
    tab_mla = rope_tables(pos, MLA_ROPE // 2)
    tab_dil = rope_tables(pos, DIL_HEAD_DIM // 2)
    tm_p, tm_d = 512, 256
    tm_s = min(tm_p, n_s)
    tm_ds = min(tm_d, n_s)
    prompt_tab = lambda tm: (lambda i: i % (seq // tm))
    sample_tab = lambda tm: (lambda i: seq // tm + i)

    kpe_pool_t = cache_mla_kpe.transpose(0, 1, 3, 2)
    dil_caches = (cache_dil_w128, cache_dil_w512, cache_dil_w2048)
    dil_caches_t = [c.transpose(0, 1, 3, 4, 5, 2).reshape(c.shape[0], c.shape[1], 2 * DIL_WIDTH, c.shape[2])
                    for c in dil_caches]
    row = lambda v: v.reshape(1, -1)

    ckv_p, kpe_p, ckv_s, kpe_s, kv_p, kv_s = [], [], [], [], [], []
    for i in range(DEPTH):
        li = i // 2
        g_mix, b_mix, g_ffn, b_ffn = row(ln_mix_g[i]), row(ln_mix_b[i]), row(ln_ffn_g[i]), row(ln_ffn_b[i])
        if i % 2 == 0:
            w = _prep_mla(mla_w_in[li], mla_q_norm[li], mla_w_uq[li], mla_kv_norm[li], mla_w_uk[li],
                          mla_w_uv[li], mla_w_o[li])
            c_p, r_p, kfull_p, q_p = mla_project(*xp, n_p, w, tab_mla, prompt_tab(tm_p), BF16, tm_p)
            o_p = mla_prompt_attention(q_p, kfull_p, batch, seq)
            xp = (mla_output(o_p, w, *xp, g_mix, b_mix, tm_p), 0)
            c_s, r_s, _, q_s = mla_project(*xs, n_s, w, tab_mla, sample_tab(tm_s), F32, tm_s)
            o_s = mla_sample_attention(q_s, c_s, r_s, cache_mla_ckv, kpe_pool_t, page_table, li, t_new)
            xs = (mla_output(o_s, w, *xs, g_mix, b_mix, tm_s), 0)
            ckv_p.append(c_p)
            kpe_p.append(r_p[:, :MLA_ROPE])
            ckv_s.append(c_s)
            kpe_s.append(r_s[:, :MLA_ROPE])
            wgu = ffn_w_gate_up[li].astype(BF16)
            wd = ffn_w_down[li].astype(BF16)
            xp = (ffn_norm(xp[0], wgu, wd, g_ffn, b_ffn, 1024, 256), 0)
            xs = (ffn_norm(xs[0], wgu, wd, g_ffn, b_ffn, tm_s, 256), 0)
        else:
            w_in = dil_w_in[li].astype(BF16)
            w_o = dil_w_o[li].astype(BF16)
            h_p, *states = dil_project(*xp, n_p, w_in, tab_dil, prompt_tab(tm_d), tm_d, state_seq=seq)
            o_p = dil_prompt_attention(h_p, batch, seq)
            h_s, = dil_project(*xs, n_s, w_in, tab_dil, sample_tab(tm_ds), tm_ds)
            o_s = dil_sample_attention(h_s.reshape(n_dec, t_new, -1), dil_caches_t, li)
            kv_p.append(states)
            kv_s.append(h_s.reshape(n_dec, t_new, len(DIL_GROUPS), 3, DIL_WIDTH)[:, :, :, 1:])
            xp_rows = lax.slice_in_dim(xp[0], xp[1], xp[1] + n_p) if xp[0].shape[0] != n_p else xp[0]
            xs_rows = lax.slice_in_dim(xs[0], xs[1], xs[1] + n_s) if xs[0].shape[0] != n_s else xs[0]
            x_all, x_chunked = proj_norm(o_p, o_s.reshape(n_s, DIL_WIDTH).astype(BF16), w_o,
                                         xp_rows, xs_rows, g_mix, b_mix, tm_s)
            router = _prep_router(moe_w_router[li], moe_b_router[li])
            y_all = moe_norm(x_all, x_chunked, *router, moe_w_gate_up[li].astype(BF16),
                             moe_w_down[li].astype(BF16), g_ffn, b_ffn)
            xp, xs = (y_all, 0), (y_all, n_p)

    xp_out = lax.slice_in_dim(xp[0], xp[1], xp[1] + n_p)
    xs_out = lax.slice_in_dim(xs[0], xs[1], xs[1] + n_s)
    outs = [xp_out.reshape(batch, seq, D_MODEL), xs_out.reshape(n_dec, t_new, D_MODEL),
            jnp.stack(ckv_p).reshape(-1, batch, seq, MLA_KV_LORA),
            jnp.stack(kpe_p).reshape(-1, batch, seq, MLA_ROPE),
            jnp.stack(ckv_s).reshape(-1, n_dec, t_new, MLA_KV_LORA),
            jnp.stack(kpe_s).reshape(-1, n_dec, t_new, MLA_ROPE)]
    state_shape = lambda b, rows: (-1, b, rows, 2, DIL_HEADS, DIL_HEAD_DIM)
    for g in range(len(DIL_GROUPS)):
        st = jnp.stack([states[g] for states in kv_p])
        st = st.reshape(st.shape[0], batch, 2, DIL_HEADS, DIL_HEAD_DIM, st.shape[-1])
        outs.append(st.transpose(0, 1, 5, 2, 3, 4))
    for g, cache in enumerate(dil_caches):
        new_rows = jnp.stack([kv[:, :, g] for kv in kv_s]).reshape(state_shape(n_dec, t_new))
        full = jnp.concatenate([cache, new_rows], axis=2)
        keep = min(DIL_GROUPS[g][0], full.shape[2])
        outs.append(full[:, :, full.shape[2] - keep:])
    return tuple(outs)
```

```python
import functools

import jax
import jax.numpy as jnp
from jax import lax
from jax.experimental import pallas as pl
from jax.experimental.pallas import tpu as pltpu

F32 = jnp.float32
BF16 = jnp.bfloat16
I32 = jnp.int32

D_MODEL = 1024
DEPTH = 4
PAGE_SIZE = 128
MLA_HEADS = 16
MLA_NOPE = 64
MLA_ROPE = 32
MLA_VDIM = 64
MLA_Q_LORA = 256
MLA_KV_LORA = 128
MLA_SCALE = (MLA_NOPE + MLA_ROPE) ** -0.5
Q_PRESCALE = MLA_SCALE * 1.4426950408889634
DIL_GROUPS = ((128, 1), (512, 4), (2048, 16))
DIL_HEADS = 8
DIL_HEAD_DIM = 64
DIL_BLOCK = 128
DIL_WIDTH = DIL_HEADS * DIL_HEAD_DIM
N_EXPERTS = 8
ROPE_THETA = 10000.0
LN_EPS = 1e-5
RMS_EPS = 1e-6
DEEPNORM_ALPHA = (2 * DEPTH) ** 0.25

LANES = 128
SUBLANES = 8
ROW_CHUNKS = D_MODEL // LANES
V7X_VMEM_BUDGET = 56 * 1024 * 1024

MOE_TILE = 256
MOE_CHUNK = 2560
GATHER_STRIDE = MOE_TILE + 1


def _params(semantics, vmem_bytes):
    return pltpu.CompilerParams(dimension_semantics=semantics,
                                vmem_limit_bytes=min(int(vmem_bytes), V7X_VMEM_BUDGET))


def _dot(a, b):
    return jnp.dot(a, b, preferred_element_type=F32)


def _dot_nt(a, b):
    return lax.dot_general(a, b, (((1,), (1,)), ((), ())), preferred_element_type=F32)


def _layer_norm(z, g, b):
    mu = jnp.mean(z, -1, keepdims=True)
    zc = z - mu
    var = jnp.mean(zc * zc, -1, keepdims=True)
    return zc * lax.rsqrt(var + LN_EPS) * g + b


def _rms_norm(h, g):
    return h * lax.rsqrt(jnp.mean(h * h, -1, keepdims=True) + RMS_EPS) * g


def _rope(x, tab_ref, half):
    n = x.shape[-1]
    return (x * tab_ref[0] + pltpu.roll(x, half, 1) * tab_ref[1]
            + pltpu.roll(x, n - half, 1) * tab_ref[2])


def _store_row_chunked(ref, y):
    rows = y.shape[0]
    for k in range(ROW_CHUNKS):
        ref[pl.ds(k, rows, stride=ROW_CHUNKS), :] = y[:, k * LANES:(k + 1) * LANES]


def _read_staging(tile_sc, rows):
    return jnp.concatenate(
        [tile_sc[k * GATHER_STRIDE:k * GATHER_STRIDE + rows, :] for k in range(ROW_CHUNKS)], -1)


def _rope_table_kernel(pos_ref, freq_ref, tab_ref, *, half):
    ang = pos_ref[...] * freq_ref[...]
    c = jnp.cos(ang)
    s = jnp.sin(ang)
    lane = lax.broadcasted_iota(I32, ang.shape, 1)
    upper = (lane % (2 * half)) >= half
    tab_ref[0] = c
    tab_ref[1] = jnp.where(upper, s, 0.0)
    tab_ref[2] = jnp.where(upper, 0.0, -s)


def rope_tables(pos, half):
    p = pos.shape[0]
    lane = jnp.arange(LANES)
    freq = (ROPE_THETA ** (-(lane % half).astype(F32) / half)).reshape(1, LANES)
    return pl.pallas_call(
        functools.partial(_rope_table_kernel, half=half),
        out_shape=jax.ShapeDtypeStruct((3, p, LANES), F32),
        name=f"rope_tables_{half}",
    )(pos.astype(F32).reshape(p, 1), freq)


def _mla_proj_kernel(x_ref, win_ref, qn_ref, kvn_ref, wuq_ref, wukt_ref, tab_ref,
                     ckv_ref, kpe_ref, kfull_ref, q_ref):
    xb = x_ref[...].astype(BF16)
    h = _dot(xb, win_ref[...])
    c_q = _rms_norm(h[:, :MLA_Q_LORA], qn_ref[...])
    c_kv = _rms_norm(h[:, MLA_Q_LORA:MLA_Q_LORA + MLA_KV_LORA], kvn_ref[...])
    kpe = _rope(h[:, MLA_Q_LORA + MLA_KV_LORA:], tab_ref, MLA_ROPE // 2)
    ckv_ref[...] = c_kv
    kpe_ref[...] = kpe
    kfull_ref[...] = jnp.concatenate([c_kv, kpe], -1).astype(BF16)
    q = _dot(c_q.astype(BF16), wuq_ref[...])
    lane = lax.broadcasted_iota(I32, (q.shape[0], LANES), 1)
    for hd in range(MLA_HEADS):
        qc = q[:, hd * LANES:(hd + 1) * LANES]
        q_lat = _dot(qc.astype(BF16), wukt_ref[hd])
        q_pe = _rope(pltpu.roll(qc, LANES - MLA_NOPE, 1), tab_ref, MLA_ROPE // 2)
        q_pe = jnp.where(lane < MLA_ROPE, q_pe, 0.0)
        q_ref[hd] = (jnp.concatenate([q_lat, q_pe], -1) * Q_PRESCALE).astype(q_ref.dtype)


def mla_project(x, row0, n, w, tab, tab_block, q_dtype, tm):
    blk0 = row0 // tm
    const = lambda *shape: pl.BlockSpec(shape, lambda i: (0,) * len(shape))
    return pl.pallas_call(
        _mla_proj_kernel,
        grid=(n // tm,),
        in_specs=[
            pl.BlockSpec((tm, D_MODEL), lambda i: (blk0 + i, 0)),
            const(D_MODEL, 512), const(1, MLA_Q_LORA), const(1, MLA_KV_LORA),
            const(MLA_Q_LORA, MLA_HEADS * LANES), const(MLA_HEADS, LANES, LANES),
            pl.BlockSpec((3, tm, LANES), lambda i: (0, tab_block(i), 0)),
        ],
        out_specs=[
            pl.BlockSpec((tm, LANES), lambda i: (i, 0)),
            pl.BlockSpec((tm, LANES), lambda i: (i, 0)),
            pl.BlockSpec((tm, 2 * LANES), lambda i: (i, 0)),
            pl.BlockSpec((MLA_HEADS, tm, 2 * LANES), lambda i: (0, i, 0)),
        ],
        out_shape=[
            jax.ShapeDtypeStruct((n, LANES), F32),
            jax.ShapeDtypeStruct((n, LANES), F32),
            jax.ShapeDtypeStruct((n, 2 * LANES), BF16),
            jax.ShapeDtypeStruct((MLA_HEADS, n, 2 * LANES), q_dtype),
        ],
        compiler_params=_params(("arbitrary",), 48 << 20),
        name="mla_project",
    )(x, w["w_in"], w["q_norm"], w["kv_norm"], w["w_uq"], w["w_ukt"], tab)


def _mla_prompt_attn_kernel(q_ref, k_ref, o_ref, m_sc, l_sc, acc_sc, *, tq, tk):
    iq = pl.program_id(1)
    rows = q_ref.shape[0] * tq
    q = q_ref[...].reshape(rows, q_ref.shape[2])
    m_sc[...] = jnp.full(m_sc.shape, -jnp.inf, F32)
    l_sc[...] = jnp.zeros(l_sc.shape, F32)
    acc_sc[...] = jnp.zeros(acc_sc.shape, F32)

    def step(kt, masked):
        k = k_ref[pl.ds(pl.multiple_of(kt * tk, tk), tk), :]
        s = _dot_nt(q, k)
        if masked:
            qpos = iq * tq + lax.broadcasted_iota(I32, s.shape, 0) % tq
            kpos = kt * tk + lax.broadcasted_iota(I32, s.shape, 1)
            s = jnp.where(kpos <= qpos, s, -jnp.inf)
        m_prev = m_sc[...]
        m_new = jnp.maximum(m_prev, jnp.max(s, -1, keepdims=True))
        alpha = jnp.exp2(m_prev - m_new)
        ps = [jnp.exp2(s[:, c * LANES:(c + 1) * LANES] - m_new) for c in range(tk // LANES)]
        l_sc[...] = alpha * l_sc[...] + sum(ps[1:], ps[0])
        acc_sc[...] = alpha * acc_sc[...] + _dot(jnp.concatenate(ps, -1).astype(BF16),
                                                 k[:, :MLA_KV_LORA])
        m_sc[...] = m_new

    n_full = (iq * tq + 1) // tk
    n_all = (iq * tq + tq - 1) // tk + 1
    lax.fori_loop(0, n_full, lambda kt, c: (step(kt, False), c)[1], 0)
    lax.fori_loop(n_full, n_all, lambda kt, c: (step(kt, True), c)[1], 0)
    o = acc_sc[...] / jnp.sum(l_sc[...], -1, keepdims=True)
    o_ref[...] = o.reshape(o_ref.shape).astype(o_ref.dtype)


def mla_prompt_attention(q, kfull, batch, seq, tq=128, tk=512):
    heads, n, dq = q.shape
    nq = seq // tq
    rows = heads * tq
    return pl.pallas_call(
        functools.partial(_mla_prompt_attn_kernel, tq=tq, tk=tk),
        grid=(batch, nq),
        in_specs=[
            pl.BlockSpec((heads, tq, dq), lambda b, i: (0, b * nq + i, 0)),
            pl.BlockSpec((seq, dq), lambda b, i: (b, 0)),
        ],
        out_specs=pl.BlockSpec((heads, tq, MLA_KV_LORA), lambda b, i: (0, b * nq + i, 0)),
        out_shape=jax.ShapeDtypeStruct((heads, n, MLA_KV_LORA), BF16),
        scratch_shapes=[pltpu.VMEM((rows, LANES), F32), pltpu.VMEM((rows, LANES), F32),
                        pltpu.VMEM((rows, MLA_KV_LORA), F32)],
        compiler_params=_params(("arbitrary", "arbitrary"), 40 << 20),
        name="mla_prompt_attention",
    )(q, kfull)


def _mla_sample_attn_kernel(pt_ref, q_ref, cnew_ref, rnew_ref, *rest, group, pages, t_new):
    n_pg = group * pages
    ckv_refs = rest[:n_pg]
    kpe_refs = rest[n_pg:2 * n_pg]
    o_ref = rest[2 * n_pg]
    scratch = rest[2 * n_pg + 1:]
    m_sc, l_sc, acc_sc = scratch[0::3], scratch[1::3], scratch[2::3]
    j = pl.program_id(1)
    rows = q_ref.shape[0] * q_ref.shape[1]
    q = q_ref[...].reshape(rows, q_ref.shape[2])
    q_lat = q[:, :MLA_KV_LORA].astype(BF16)
    q_pe = q[:, MLA_KV_LORA:MLA_KV_LORA + MLA_ROPE].astype(BF16)

    @pl.when(j == 0)
    def _():
        for bb in range(group):
            m_sc[bb][...] = jnp.full(m_sc[bb].shape, -jnp.inf, F32)
            l_sc[bb][...] = jnp.zeros(l_sc[bb].shape, F32)
            acc_sc[bb][...] = jnp.zeros(acc_sc[bb].shape, F32)

    def update(bb, s, v):
        m_prev = m_sc[bb][...]
        m_new = jnp.maximum(m_prev, jnp.max(s, -1, keepdims=True))
        alpha = jnp.exp2(m_prev - m_new)
        p = jnp.exp2(s - m_new[:, :1])
        l_sc[bb][...] = alpha * l_sc[bb][...] + jnp.sum(p, -1, keepdims=True)
        acc_sc[bb][...] = alpha * acc_sc[bb][...] + _dot(p.astype(BF16), v)
        m_sc[bb][...] = m_new

    for bb in range(group):
        ck = jnp.concatenate([ckv_refs[bb * pages + p][...] for p in range(pages)], 0).astype(BF16)
        kp_t = jnp.concatenate([kpe_refs[bb * pages + p][...] for p in range(pages)], 1).astype(BF16)
        update(bb, _dot_nt(q_lat, ck) + _dot(q_pe, kp_t), ck)

    @pl.when(j == pl.num_programs(1) - 1)
    def _():
        cn = cnew_ref[...].astype(BF16)
        rn = rnew_ref[...][:, :MLA_ROPE].astype(BF16)
        s_new = _dot_nt(q_lat, cn) + _dot_nt(q_pe, rn)
        tok_q = lax.broadcasted_iota(I32, s_new.shape, 0) % (group * t_new)
        tok_k = lax.broadcasted_iota(I32, s_new.shape, 1)
        out = jnp.zeros(acc_sc[0].shape, F32)
        for bb in range(group):
            ok = (tok_k // t_new == bb) & (tok_k % t_new <= tok_q % t_new)
            update(bb, jnp.where(ok, s_new, -jnp.inf), cn)
            sel = lax.broadcasted_iota(I32, out.shape, 0) % (group * t_new) // t_new == bb
            out = jnp.where(sel, acc_sc[bb][...] / l_sc[bb][...], out)
        o_ref[...] = out.reshape(o_ref.shape)


def mla_sample_attention(q, ckv_new, kpe_new, ckv_pool, kpe_pool_t, page_table, layer, t_new,
                         group=2, pages=16):
    heads, n, dq = q.shape
    n_batch, n_pages = page_table.shape
    pages = min(pages, n_pages)
    gt = group * t_new
    rows = heads * gt

    def pool_spec(bb, p, shape):
        def index(g, j, pt):
            return (layer, pt[(g * group + bb) * n_pages + j * pages + p], 0, 0)
        return pl.BlockSpec((None, None) + shape, index)

    in_specs = [
        pl.BlockSpec((heads, gt, dq), lambda g, j, pt: (0, g, 0)),
        pl.BlockSpec((gt, LANES), lambda g, j, pt: (g, 0)),
        pl.BlockSpec((gt, LANES), lambda g, j, pt: (g, 0)),
    ]
    in_specs += [pool_spec(bb, p, (PAGE_SIZE, MLA_KV_LORA)) for bb in range(group) for p in range(pages)]
    in_specs += [pool_spec(bb, p, (MLA_ROPE, PAGE_SIZE)) for bb in range(group) for p in range(pages)]
    grid_spec = pltpu.PrefetchScalarGridSpec(
        num_scalar_prefetch=1,
        grid=(n_batch // group, n_pages // pages),
        in_specs=in_specs,
        out_specs=pl.BlockSpec((heads, gt, MLA_KV_LORA), lambda g, j, pt: (0, g, 0)),
        scratch_shapes=[pltpu.VMEM((rows, LANES), F32), pltpu.VMEM((rows, LANES), F32),
                        pltpu.VMEM((rows, MLA_KV_LORA), F32)] * group,
    )
    n_pg = group * pages
    return pl.pallas_call(
        functools.partial(_mla_sample_attn_kernel, group=group, pages=pages, t_new=t_new),
        grid_spec=grid_spec,
        out_shape=jax.ShapeDtypeStruct((heads, n, MLA_KV_LORA), F32),
        compiler_params=_params(("arbitrary", "arbitrary"), 32 << 20),
        name="mla_sample_attention",
    )(page_table.reshape(-1), q, ckv_new, kpe_new, *([ckv_pool] * n_pg), *([kpe_pool_t] * n_pg))


def _mla_out_kernel(o_ref, wuv_ref, wo_ref, x_ref, g_ref, b_ref, y_ref):
    parts = []
    for p in range(MLA_HEADS // 2):
        a = jnp.concatenate([o_ref[2 * p], o_ref[2 * p + 1]], -1).astype(BF16)
        parts.append(_dot(a, wuv_ref[p]).astype(BF16))
    v = jnp.concatenate(parts, -1)
    z = DEEPNORM_ALPHA * x_ref[...] + _dot(v, wo_ref[...])
    y_ref[...] = _layer_norm(z, g_ref[...], b_ref[...])


def mla_output(o_lat, w, x, row0, g, b, tm):
    heads, n, c = o_lat.shape
    blk0 = row0 // tm
    const = lambda *shape: pl.BlockSpec(shape, lambda i: (0,) * len(shape))
    return pl.pallas_call(
        _mla_out_kernel,
        grid=(n // tm,),
        in_specs=[
            pl.BlockSpec((heads, tm, c), lambda i: (0, i, 0)),
            const(heads // 2, 2 * c, LANES), const(heads * MLA_VDIM, D_MODEL),
            pl.BlockSpec((tm, D_MODEL), lambda i: (blk0 + i, 0)),
            const(1, D_MODEL), const(1, D_MODEL),
        ],
        out_specs=pl.BlockSpec((tm, D_MODEL), lambda i: (i, 0)),
        out_shape=jax.ShapeDtypeStruct((n, D_MODEL), F32),
        compiler_params=_params(("arbitrary",), 40 << 20),
        name="mla_output",
    )(o_lat, w["w_uv_pair"], w["w_o"], x, g, b)


def _proj_norm_kernel(ap_ref, as_ref, wo_ref, xp_ref, xs_ref, g_ref, b_ref, y_ref, yc_ref, *, n_prompt_tiles):
    def run(a_ref, x_ref):
        z = DEEPNORM_ALPHA * x_ref[...] + _dot(a_ref[...].astype(BF16), wo_ref[...])
        y = _layer_norm(z, g_ref[...], b_ref[...])
        y_ref[...] = y
        _store_row_chunked(yc_ref, y)

    is_prompt = pl.program_id(0) < n_prompt_tiles
    pl.when(is_prompt)(lambda: run(ap_ref, xp_ref))
    pl.when(jnp.logical_not(is_prompt))(lambda: run(as_ref, xs_ref))


def proj_norm(a_p, a_s, w_o, xp, xs, g, b, tm):
    n_p, k = a_p.shape
    n_s = a_s.shape[0]
    tp, ts = n_p // tm, n_s // tm
    n = n_p + n_s
    const = lambda *shape: pl.BlockSpec(shape, lambda i: (0,) * len(shape))
    p_idx = lambda i: (jnp.minimum(i, tp - 1), 0)
    s_idx = lambda i: (jnp.maximum(i - tp, 0), 0)
    return pl.pallas_call(
        functools.partial(_proj_norm_kernel, n_prompt_tiles=tp),
        grid=(tp + ts,),
        in_specs=[
            pl.BlockSpec((tm, k), p_idx), pl.BlockSpec((tm, k), s_idx), const(k, D_MODEL),
            pl.BlockSpec((tm, D_MODEL), p_idx), pl.BlockSpec((tm, D_MODEL), s_idx),
            const(1, D_MODEL), const(1, D_MODEL),
        ],
        out_specs=[pl.BlockSpec((tm, D_MODEL), lambda i: (i, 0)),
                   pl.BlockSpec((tm * ROW_CHUNKS, LANES), lambda i: (i, 0))],
        out_shape=[jax.ShapeDtypeStruct((n, D_MODEL), F32),
                   jax.ShapeDtypeStruct((n * ROW_CHUNKS, LANES), F32)],
        compiler_params=_params(("arbitrary",), 40 << 20),
        name="proj_norm",
    )(a_p, a_s, w_o, xp, xs, g, b)


def _ffn_kernel(x_ref, wg_ref, wu_ref, wd_ref, g_ref, b_ref, y_ref, xb_sc, acc_sc):
    f = pl.program_id(1)

    @pl.when(f == 0)
    def _():
        xb_sc[...] = x_ref[...].astype(BF16)

    xb = xb_sc[...]
    hg = _dot(xb, wg_ref[...])
    hu = _dot(xb, wu_ref[...])
    act = (hg / (1.0 + jnp.exp(-hg)) * hu).astype(BF16)
    part = _dot(act, wd_ref[...])

    @pl.when(f == 0)
    def _():
        acc_sc[...] = part

    @pl.when(f > 0)
    def _():
        acc_sc[...] += part

    @pl.when(f == pl.num_programs(1) - 1)
    def _():
        z = DEEPNORM_ALPHA * x_ref[...] + acc_sc[...]
        y_ref[...] = _layer_norm(z, g_ref[...], b_ref[...])


def ffn_norm(x, w_gate_up, w_down, g, b, tm, tf):
    n = x.shape[0]
    d_ff = w_down.shape[0]
    nf = d_ff // tf
    const = lambda *shape: pl.BlockSpec(shape, lambda i, f: (0,) * len(shape))
    return pl.pallas_call(
        _ffn_kernel,
        grid=(n // tm, nf),
        in_specs=[
            pl.BlockSpec((tm, D_MODEL), lambda i, f: (i, 0)),
            pl.BlockSpec((D_MODEL, tf), lambda i, f: (0, f)),
            pl.BlockSpec((D_MODEL, tf), lambda i, f: (0, nf + f)),
            pl.BlockSpec((tf, D_MODEL), lambda i, f: (f, 0)),
            const(1, D_MODEL), const(1, D_MODEL),
        ],
        out_specs=pl.BlockSpec((tm, D_MODEL), lambda i, f: (i, 0)),
        out_shape=jax.ShapeDtypeStruct((n, D_MODEL), F32),
        scratch_shapes=[pltpu.VMEM((tm, D_MODEL), BF16), pltpu.VMEM((tm, D_MODEL), F32)],
        compiler_params=_params(("arbitrary", "arbitrary"), 48 << 20),
        name="ffn_norm",
    )(x, w_gate_up, w_gate_up, w_down, g, b)


def _moe_route_kernel(x_ref, wr_ref, br_ref, pos_ref, gate_ref, te_ref, *, n_exp):
    x = x_ref[...]
    c = x.shape[0]
    x1 = x.astype(BF16)
    x2 = (x - x1.astype(F32)).astype(BF16)
    logits = (_dot(x1, wr_ref[0]) + _dot(x1, wr_ref[1]) + _dot(x2, wr_ref[0])
              + _dot(x2, wr_ref[1]) + br_ref[...])
    lane = lax.broadcasted_iota(I32, logits.shape, 1)
    logits = jnp.where(lane < n_exp, logits, -jnp.inf)
    v1 = jnp.max(logits, -1, keepdims=True)
    i1 = jnp.min(jnp.where(logits == v1, lane, LANES), -1, keepdims=True)
    sel1 = lane == i1
    rest = jnp.where(sel1, -jnp.inf, logits)
    v2 = jnp.max(rest, -1, keepdims=True)
    i2 = jnp.min(jnp.where(rest == v2, lane, LANES), -1, keepdims=True)
    sel2 = lane == i2
    e2 = jnp.exp(v2 - v1)
    den = 1.0 + e2
    gate_ref[...] = jnp.where(lane == 0, 1.0 / den, jnp.where(lane == 1, e2 / den, 0.0))

    onehot = jnp.where(sel1 | sel2, 1.0, 0.0).astype(BF16)
    blk = MOE_TILE
    r_i = lax.broadcasted_iota(I32, (blk, blk), 0)
    c_i = lax.broadcasted_iota(I32, (blk, blk), 1)
    below = jnp.where(c_i < r_i, 1.0, 0.0).astype(BF16)
    carry = jnp.zeros((1, LANES), F32)
    ranks = []
    for t in range(c // blk):
        ob = onehot[t * blk:(t + 1) * blk]
        ranks.append(_dot(below, ob) + carry)
        carry = carry + jnp.sum(ob.astype(F32), 0, keepdims=True)
    rank = jnp.concatenate(ranks, 0)
    n_tiles = jnp.floor((carry + (blk - 1.0)) * (1.0 / blk))
    e_r = lax.broadcasted_iota(I32, (LANES, LANES), 0)
    e_c = lax.broadcasted_iota(I32, (LANES, LANES), 1)
    before = jnp.where(e_r < e_c, 1.0, 0.0).astype(BF16)
    first_tile = _dot(jnp.broadcast_to(n_tiles, (SUBLANES, LANES)).astype(BF16), before)[:1]
    slot = first_tile * float(blk) + rank
    pos1 = jnp.sum(jnp.where(sel1, slot, 0.0), -1, keepdims=True)
    pos2 = jnp.sum(jnp.where(sel2, slot, 0.0), -1, keepdims=True)
    pos_ref[...] = jnp.where(lane == 0, pos1, jnp.where(lane == 1, pos2, 0.0)).astype(I32)

    tile_id = lax.broadcasted_iota(I32, te_ref.shape, 0).astype(F32)
    lane_t = lax.broadcasted_iota(I32, te_ref.shape, 1)
    ended = (first_tile + n_tiles <= tile_id) & (lane_t < n_exp)
    te = jnp.sum(jnp.where(ended, 1.0, 0.0), -1, keepdims=True)
    te_ref[...] = jnp.broadcast_to(te, te_ref.shape).astype(I32)


def moe_route(x, w_router, b_router, n_exp, chunk, tiles_per_chunk):
    n = x.shape[0]
    n_chunks = n // chunk
    te_rows = -(-tiles_per_chunk // SUBLANES) * SUBLANES
    const = lambda *shape: pl.BlockSpec(shape, lambda c: (0,) * len(shape))
    return pl.pallas_call(
        functools.partial(_moe_route_kernel, n_exp=n_exp),
        grid=(n_chunks,),
        in_specs=[pl.BlockSpec((chunk, D_MODEL), lambda c: (c, 0)),
                  const(2, D_MODEL, LANES), const(1, LANES)],
        out_specs=[pl.BlockSpec((chunk, LANES), lambda c: (c, 0)),
                   pl.BlockSpec((chunk, LANES), lambda c: (c, 0)),
                   pl.BlockSpec((te_rows, LANES), lambda c: (c, 0))],
        out_shape=[jax.ShapeDtypeStruct((n, LANES), I32), jax.ShapeDtypeStruct((n, LANES), F32),
                   jax.ShapeDtypeStruct((n_chunks * te_rows, LANES), I32)],
        compiler_params=_params(("arbitrary",), 48 << 20),
        name="moe_route",
    )(x, w_router, b_router)


def _moe_dispatch_kernel(pos_ref, xc_ref, o_ref, src_sm, tile_sc, *, rows_per_chunk):
    j = pl.program_id(1)
    n_tok = pos_ref.shape[1]

    @pl.when((j == 0) & (pl.program_id(0) == 0))
    def _():
        def clear(i, c):
            src_sm[i] = 0
            return c
        lax.fori_loop(0, rows_per_chunk, clear, 0, unroll=8)

    @pl.when(j == 0)
    def _():
        def invert(t, c):
            src_sm[pos_ref[0, t]] = t
            src_sm[pos_ref[1, t]] = t
            return c
        lax.fori_loop(0, n_tok, invert, 0, unroll=8)

    base = j * MOE_TILE
    for r in range(MOE_TILE):
        tok = src_sm[base + r]
        row = xc_ref[pl.ds(pl.multiple_of(tok * ROW_CHUNKS, ROW_CHUNKS), ROW_CHUNKS), :]
        tile_sc[pl.ds(r, ROW_CHUNKS, stride=GATHER_STRIDE), :] = row
    o_ref[...] = _read_staging(tile_sc, MOE_TILE).astype(BF16)


def moe_dispatch(pos_t, x_chunked, tiles_per_chunk):
    n_chunks, _, chunk = pos_t.shape
    rows_per_chunk = tiles_per_chunk * MOE_TILE
    return pl.pallas_call(
        functools.partial(_moe_dispatch_kernel, rows_per_chunk=rows_per_chunk),
        grid=(n_chunks, tiles_per_chunk),
        in_specs=[
            pl.BlockSpec((None, 2, chunk), lambda c, j: (c, 0, 0), memory_space=pltpu.SMEM),
            pl.BlockSpec((chunk * ROW_CHUNKS, LANES), lambda c, j: (c, 0)),
        ],
        out_specs=pl.BlockSpec((MOE_TILE, D_MODEL), lambda c, j: (c * tiles_per_chunk + j, 0)),
        out_shape=jax.ShapeDtypeStruct((n_chunks * rows_per_chunk, D_MODEL), BF16),
        scratch_shapes=[pltpu.SMEM((rows_per_chunk,), I32),
                        pltpu.VMEM((ROW_CHUNKS * GATHER_STRIDE, LANES), F32)],
        compiler_params=_params(("arbitrary", "arbitrary"), 40 << 20),
        name="moe_dispatch",
    )(pos_t, x_chunked)


def _moe_expert_kernel(order_ref, expert_ref, nact_ref, x_ref, wgu_ref, wd_ref, o_ref, *, tf):
    i = pl.program_id(0)
    d_ff = wd_ref.shape[0]

    @pl.when(i < nact_ref[0])
    def _():
        x = x_ref[...]
        acc = jnp.zeros((x.shape[0], D_MODEL), F32)
        for f in range(d_ff // tf):
            hg = _dot(x, wgu_ref[:, f * tf:(f + 1) * tf])
            hu = _dot(x, wgu_ref[:, d_ff + f * tf:d_ff + (f + 1) * tf])
            act = (hg / (1.0 + jnp.exp(-hg)) * hu).astype(BF16)
            acc = acc + _dot(act, wd_ref[f * tf:(f + 1) * tf, :])
        _store_row_chunked(o_ref, acc)

    @pl.when(i >= nact_ref[0])
    def _():
        o_ref[...] = jnp.zeros(o_ref.shape, F32)


def moe_experts(order, tile_expert, n_active, xs, w_gate_up, w_down, layer, tf=512):
    n_tiles = order.shape[0]
    d_ff = w_down.shape[2]
    grid_spec = pltpu.PrefetchScalarGridSpec(
        num_scalar_prefetch=3,
        grid=(n_tiles,),
        in_specs=[
            pl.BlockSpec((MOE_TILE, D_MODEL), lambda i, o, e, na: (o[i], 0)),
            pl.BlockSpec((None, None, D_MODEL, 2 * d_ff), lambda i, o, e, na: (layer, e[i], 0, 0),
                         pipeline_mode=pl.Buffered(1)),
            pl.BlockSpec((None, None, d_ff, D_MODEL), lambda i, o, e, na: (layer, e[i], 0, 0),
                         pipeline_mode=pl.Buffered(1)),
        ],
        out_specs=pl.BlockSpec((MOE_TILE * ROW_CHUNKS, LANES), lambda i, o, e, na: (o[i], 0)),
    )
    return pl.pallas_call(
        functools.partial(_moe_expert_kernel, tf=tf),
        grid_spec=grid_spec,
        out_shape=jax.ShapeDtypeStruct((n_tiles * MOE_TILE * ROW_CHUNKS, LANES), F32),
        compiler_params=_params(("arbitrary",), 48 << 20),
        name="moe_experts",
    )(order, tile_expert, n_active, xs, w_gate_up, w_down)


def _moe_combine_kernel(pos_ref, gate_ref, ys_ref, x_ref, g_ref, b_ref, y_ref, tile_sc):
    j = pl.program_id(1)
    tm = x_ref.shape[0]
    for r in range(tm):
        t = j * tm + r
        r0 = ys_ref[pl.ds(pl.multiple_of(pos_ref[0, t] * ROW_CHUNKS, ROW_CHUNKS), ROW_CHUNKS), :]
        r1 = ys_ref[pl.ds(pl.multiple_of(pos_ref[1, t] * ROW_CHUNKS, ROW_CHUNKS), ROW_CHUNKS), :]
        tile_sc[pl.ds(r, ROW_CHUNKS, stride=GATHER_STRIDE), :] = gate_ref[0, t] * r0 + gate_ref[1, t] * r1
    z = DEEPNORM_ALPHA * x_ref[...] + _read_staging(tile_sc, tm)
    y_ref[...] = _layer_norm(z, g_ref[...], b_ref[...])


def moe_combine(pos_t, gate_t, ys_chunked, x, g, b, tiles_per_chunk):
    n = x.shape[0]
    n_chunks, _, chunk = pos_t.shape
    tm = MOE_TILE
    per_chunk = chunk // tm
    rows_per_chunk = tiles_per_chunk * MOE_TILE
    smem = lambda: pl.BlockSpec((None, 2, chunk), lambda c, j: (c, 0, 0), memory_space=pltpu.SMEM)
    const = lambda *shape: pl.BlockSpec(shape, lambda c, j: (0,) * len(shape))
    return pl.pallas_call(
        _moe_combine_kernel,
        grid=(n_chunks, per_chunk),
        in_specs=[
            smem(), smem(),
            pl.BlockSpec((rows_per_chunk * ROW_CHUNKS, LANES), lambda c, j: (c, 0),
                         pipeline_mode=pl.Buffered(1)),
            pl.BlockSpec((tm, D_MODEL), lambda c, j: (c * per_chunk + j, 0)),
            const(1, D_MODEL), const(1, D_MODEL),
        ],
        out_specs=pl.BlockSpec((tm, D_MODEL), lambda c, j: (c * per_chunk + j, 0)),
        out_shape=jax.ShapeDtypeStruct((n, D_MODEL), F32),
        scratch_shapes=[pltpu.VMEM((ROW_CHUNKS * GATHER_STRIDE, LANES), F32)],
        compiler_params=_params(("arbitrary", "arbitrary"), 48 << 20),
        name="moe_combine",
    )(pos_t, gate_t, ys_chunked, x, g, b)


def moe_norm(x, x_chunked, w_router, b_router, w_gate_up, w_down, layer, g, b):
    n = x.shape[0]
    n_exp = w_down.shape[1]
    chunk = max(c for c in range(MOE_TILE, MOE_CHUNK + 1, MOE_TILE) if n % c == 0)
    n_chunks = n // chunk
    tiles_per_chunk = 2 * chunk // MOE_TILE + n_exp
    pos, gate, te = moe_route(x, w_router, b_router, n_exp, chunk, tiles_per_chunk)
    to_smem = lambda a: a[:, :2].reshape(n_chunks, chunk, 2).transpose(0, 2, 1)
    pos_t, gate_t = to_smem(pos), to_smem(gate)
    tile_e = te.reshape(n_chunks, -1, LANES)[:, :tiles_per_chunk, 0].reshape(-1)
    order = jnp.argsort(tile_e, stable=True).astype(I32)
    sorted_e = tile_e[order]
    n_active = jnp.sum(tile_e < n_exp).astype(I32)
    last_e = sorted_e[jnp.maximum(n_active - 1, 0)]
    sorted_e = jnp.where(sorted_e < n_exp, sorted_e, last_e).astype(I32)
    xs = moe_dispatch(pos_t, x_chunked, tiles_per_chunk)
    ys = moe_experts(order, sorted_e, n_active.reshape(1), xs, w_gate_up, w_down, layer)
    return moe_combine(pos_t, gate_t, ys, x, g, b, tiles_per_chunk)


def _dil_proj_kernel(x_ref, w_ref, tab_ref, h_ref, *state_refs):
    xb = x_ref[...].astype(BF16)
    tm = xb.shape[0]
    gw = 3 * DIL_WIDTH
    for g in range(len(DIL_GROUPS)):
        hg = _dot(xb, w_ref[:, g * gw:(g + 1) * gw])
        roped = []
        for c in range(2 * DIL_WIDTH // LANES):
            roped.append(_rope(hg[:, c * LANES:(c + 1) * LANES], tab_ref, DIL_HEAD_DIM // 2))
            h_ref[:, g * gw + c * LANES:g * gw + (c + 1) * LANES] = roped[-1]
        h_ref[:, g * gw + 2 * DIL_WIDTH:(g + 1) * gw] = hg[:, 2 * DIL_WIDTH:]
        if state_refs:
            kv_t = jnp.transpose(jnp.concatenate(roped[DIL_WIDTH // LANES:] + [hg[:, 2 * DIL_WIDTH:]], -1))
            state_refs[g][...] = kv_t[:, tm - state_refs[g].shape[1]:]


def dil_project(x, row0, n, w_in, tab, tab_block, tm, state_seq=None):
    blk0 = row0 // tm
    width = w_in.shape[1]
    out_specs = [pl.BlockSpec((tm, width), lambda i: (i, 0))]
    out_shape = [jax.ShapeDtypeStruct((n, width), F32)]
    if state_seq is not None:
        per_batch = state_seq // tm
        for window, _ in DIL_GROUPS:
            keep = min(window, state_seq)
            wblk = min(keep, tm)
            first = (state_seq - keep) // tm
            out_specs.append(pl.BlockSpec(
                (None, 2 * DIL_WIDTH, wblk),
                lambda i, first=first: (i // per_batch, 0, jnp.maximum(i % per_batch - first, 0))))
            out_shape.append(jax.ShapeDtypeStruct((n // state_seq, 2 * DIL_WIDTH, keep), F32))
    return pl.pallas_call(
        _dil_proj_kernel,
        grid=(n // tm,),
        in_specs=[
            pl.BlockSpec((tm, D_MODEL), lambda i: (blk0 + i, 0)),
            pl.BlockSpec((D_MODEL, width), lambda i: (0, 0)),
            pl.BlockSpec((3, tm, LANES), lambda i: (0, tab_block(i), 0)),
        ],
        out_specs=out_specs,
        out_shape=out_shape,
        compiler_params=_params(("arbitrary",), 52 << 20),
        name="dil_project",
    )(x, w_in, tab)


def _dil_prompt_kernel(*refs, seq):
    n_g = len(DIL_GROUPS)
    qkv = refs[:3 * n_g]
    o_ref = refs[3 * n_g]
    dq_sc, dk_sc, dv_sc, rm_sc, rl_sc, ra_sc = refs[3 * n_g + 1:3 * n_g + 7]
    stats = refs[3 * n_g + 7:]
    blk = DIL_BLOCK
    n_blk = seq // blk
    lane = lax.broadcasted_iota(I32, (blk, LANES), 1)
    head0 = lane < DIL_HEAD_DIM
    qi = lax.broadcasted_iota(I32, (2 * blk, blk), 0) % blk
    ki = lax.broadcasted_iota(I32, (2 * blk, blk), 1)
    scale = DIL_HEAD_DIM ** -0.5

    for g, (window, dil) in enumerate(DIL_GROUPS):
        q_ref, k_ref, v_ref = qkv[3 * g:3 * g + 3]
        m_ref, l_ref, a_ref = stats[3 * g:3 * g + 3]
        sub = seq // dil
        per_res = sub // blk
        if dil > 1:
            for r in range(dil):
                dq_sc[r * sub:(r + 1) * sub, :] = q_ref[pl.ds(r, sub, stride=dil), :]
                dk_sc[r * sub:(r + 1) * sub, :] = k_ref[pl.ds(r, sub, stride=dil), :]
                dv_sc[r * sub:(r + 1) * sub, :] = v_ref[pl.ds(r, sub, stride=dil), :]
            qs, ks, vs = dq_sc, dk_sc, dv_sc
            ms, ls, accs = rm_sc, rl_sc, ra_sc
        else:
            qs, ks, vs = q_ref, k_ref, v_ref
            ms, ls, accs = m_ref, l_ref, a_ref

        def block(j, carry, qs=qs, ks=ks, vs=vs, ms=ms, ls=ls, accs=accs, per_res=per_res):
            cur = pl.ds(pl.multiple_of(j * blk, blk), blk)
            q = qs[cur, :]
            q2 = jnp.concatenate([jnp.where(head0, q, 0.0), jnp.where(head0, 0.0, q)], 0).astype(BF16)
            k_cur = ks[cur, :].astype(BF16)
            v_cur = vs[cur, :].astype(BF16)
            s_cur = jnp.where(ki <= qi, _dot_nt(q2, k_cur) * scale, -jnp.inf)
            if per_res > 1:
                has_prev = (j % per_res) > 0
                prev = pl.ds(pl.multiple_of(jnp.maximum(j - 1, 0) * blk, blk), blk)
                k_prev = ks[prev, :].astype(BF16)
                v_prev = vs[prev, :].astype(BF16)
                s_prev = jnp.where((ki >= qi) & has_prev, _dot_nt(q2, k_prev) * scale, -jnp.inf)
                m = jnp.maximum(jnp.max(s_cur, -1, keepdims=True), jnp.max(s_prev, -1, keepdims=True))
                p_prev = jnp.exp(s_prev - m)
                p_cur = jnp.exp(s_cur - m)
                l = jnp.sum(p_cur, -1, keepdims=True) + jnp.sum(p_prev, -1, keepdims=True)
                acc = _dot(p_cur.astype(BF16), v_cur) + _dot(p_prev.astype(BF16), v_prev)
            else:
                m = jnp.max(s_cur, -1, keepdims=True)
                p_cur = jnp.exp(s_cur - m)
                l = jnp.sum(p_cur, -1, keepdims=True)
                acc = _dot(p_cur.astype(BF16), v_cur)
            ms[cur, :] = jnp.where(head0, m[:blk], m[blk:])
            ls[cur, :] = jnp.where(head0, l[:blk], l[blk:])
            accs[cur, :] = jnp.where(head0, acc[:blk], acc[blk:])
            return carry

        lax.fori_loop(0, n_blk, block, 0, unroll=4)
        if dil > 1:
            for r in range(dil):
                m_ref[pl.ds(r, sub, stride=dil), :] = rm_sc[r * sub:(r + 1) * sub, :]
                l_ref[pl.ds(r, sub, stride=dil), :] = rl_sc[r * sub:(r + 1) * sub, :]
                a_ref[pl.ds(r, sub, stride=dil), :] = ra_sc[r * sub:(r + 1) * sub, :]

    m_all = stats[0][...]
    for g in range(1, n_g):
        m_all = jnp.maximum(m_all, stats[3 * g][...])
    num = jnp.zeros(o_ref.shape, F32)
    den = jnp.zeros(o_ref.shape, F32)
    for g in range(n_g):
        wgt = jnp.exp(stats[3 * g][...] - m_all)
        num += wgt * stats[3 * g + 2][...]
        den += wgt * stats[3 * g + 1][...]
    o_ref[...] = (num / den).astype(o_ref.dtype)


def dil_prompt_attention(h, batch, seq):
    n = h.shape[0]
    for window, dil in DIL_GROUPS:
        assert window // dil == DIL_BLOCK and seq % (dil * DIL_BLOCK) == 0
    pairs = DIL_WIDTH // LANES
    per_group = 3 * pairs
    in_specs = []
    for g in range(len(DIL_GROUPS)):
        for part in range(3):
            in_specs.append(pl.BlockSpec(
                (seq, LANES), lambda b, hp, g=g, part=part: (b, g * per_group + part * pairs + hp)))
    slab = pltpu.VMEM((seq, LANES), F32)
    return pl.pallas_call(
        functools.partial(_dil_prompt_kernel, seq=seq),
        grid=(batch, pairs),
        in_specs=in_specs,
        out_specs=pl.BlockSpec((seq, LANES), lambda b, hp: (b, hp)),
        out_shape=jax.ShapeDtypeStruct((n, DIL_WIDTH), BF16),
        scratch_shapes=[slab] * (6 + 3 * len(DIL_GROUPS)),
        compiler_params=_params(("arbitrary", "arbitrary"), 48 << 20),
        name="dil_prompt_attention",
    )(*([h] * (3 * len(DIL_GROUPS))))


def _dil_sample_kernel(h_ref, *refs, t_new):
    n_g = len(DIL_GROUPS)
    bufs = refs[:n_g]
    o_ref = refs[n_g]
    gw = 3 * DIL_WIDTH
    rows = t_new * DIL_HEADS
    row = lax.broadcasted_iota(I32, (rows, DIL_WIDTH), 0)
    col_head = lax.broadcasted_iota(I32, (rows, DIL_WIDTH), 1) // DIL_HEAD_DIM
    own = col_head == row % DIL_HEADS
    scale = DIL_HEAD_DIM ** -0.5
    scores, values = [], []
    for g, (window, dil) in enumerate(DIL_GROUPS):
        buf = bufs[g]
        n_buf = buf.shape[1]
        hq = h_ref[:, g * gw:g * gw + DIL_WIDTH]
        q = jnp.concatenate(
            [jnp.broadcast_to(hq[t:t + 1, :], (DIL_HEADS, DIL_WIDTH)) for t in range(t_new)], 0)
        q = jnp.where(own, q, 0.0).astype(BF16)
        k_old_t = buf[:DIL_WIDTH, :].astype(BF16)
        k_new = h_ref[:, g * gw + DIL_WIDTH:g * gw + 2 * DIL_WIDTH].astype(BF16)
        for s, first in ((_dot(q, k_old_t), 0), (_dot_nt(q, k_new), n_buf)):
            pos = first + lax.broadcasted_iota(I32, s.shape, 1)
            back = n_buf + lax.broadcasted_iota(I32, s.shape, 0) // DIL_HEADS - pos
            ok = (back >= 0) & (back % dil == 0) & (back <= window)
            scores.append(jnp.where(ok, s * scale, -jnp.inf))
        values.append((buf[DIL_WIDTH:, :].astype(BF16), True))
        values.append((h_ref[:, g * gw + 2 * DIL_WIDTH:(g + 1) * gw].astype(BF16), False))
    m = scores[0].max(-1, keepdims=True)
    for s in scores[1:]:
        m = jnp.maximum(m, s.max(-1, keepdims=True))
    den = jnp.zeros((rows, 1), F32)
    num = jnp.zeros((rows, DIL_WIDTH), F32)
    for s, (v, transposed) in zip(scores, values):
        p = jnp.exp(s - m)
        den += p.sum(-1, keepdims=True)
        num += _dot_nt(p.astype(BF16), v) if transposed else _dot(p.astype(BF16), v)
    o = jnp.where(own, num / den, 0.0)
    o_ref[...] = jnp.concatenate(
        [jnp.sum(o[t * DIL_HEADS:(t + 1) * DIL_HEADS], 0, keepdims=True) for t in range(t_new)], 0)


def dil_sample_attention(h, caches, layer):
    n_batch, t_new, width = h.shape
    in_specs = [pl.BlockSpec((None, t_new, width), lambda b: (b, 0, 0))]
    for c in caches:
        in_specs.append(pl.BlockSpec((None, None, c.shape[2], c.shape[3]), lambda b: (layer, b, 0, 0)))
    return pl.pallas_call(
        functools.partial(_dil_sample_kernel, t_new=t_new),
        grid=(n_batch,),
        in_specs=in_specs,
        out_specs=pl.BlockSpec((None, t_new, DIL_WIDTH), lambda b: (b, 0, 0)),
        out_shape=jax.ShapeDtypeStruct((n_batch, t_new, DIL_WIDTH), F32),
        compiler_params=_params(("arbitrary",), 48 << 20),
        name="dil_sample_attention",
    )(h, *caches)


def _prep_mla(w_in, q_norm, w_uq, kv_norm, w_uk, w_uv, w_o):
    pad_in = 512 - w_in.shape[1]
    w_uq_h = w_uq.reshape(MLA_Q_LORA, MLA_HEADS, MLA_NOPE + MLA_ROPE)
    w_uq_h = jnp.pad(w_uq_h, ((0, 0), (0, 0), (0, LANES - MLA_NOPE - MLA_ROPE)))
    w_ukt = jnp.pad(w_uk.transpose(1, 2, 0), ((0, 0), (0, LANES - MLA_NOPE), (0, 0)))
    w_uv_h = w_uv.transpose(1, 0, 2)
    zero = jnp.zeros_like(w_uv_h[0::2])
    w_uv_pair = jnp.concatenate([jnp.concatenate([w_uv_h[0::2], zero], 2),
                                 jnp.concatenate([zero, w_uv_h[1::2]], 2)], 1)
    return {
        "w_in": jnp.pad(w_in, ((0, 0), (0, pad_in))).astype(BF16),
        "q_norm": q_norm.reshape(1, -1), "kv_norm": kv_norm.reshape(1, -1),
        "w_uq": w_uq_h.reshape(MLA_Q_LORA, MLA_HEADS * LANES).astype(BF16),
        "w_ukt": w_ukt.astype(BF16), "w_uv_pair": w_uv_pair.astype(BF16), "w_o": w_o.astype(BF16),
    }


def _prep_router(w_router, b_router):
    w = jnp.pad(w_router, ((0, 0), (0, LANES - w_router.shape[1])))
    hi = w.astype(BF16)
    lo = (w - hi.astype(F32)).astype(BF16)
    return jnp.stack([hi, lo]), jnp.pad(b_router, (0, LANES - b_router.shape[0])).reshape(1, LANES)


def kernel(x_prompt, x_sample, cache_mla_ckv, cache_mla_kpe, cache_dil_w128, cache_dil_w512,
           cache_dil_w2048, page_table, mla_w_in, mla_q_norm, mla_w_uq, mla_kv_norm, mla_w_uk,
           mla_w_uv, mla_w_o, dil_w_in, dil_w_o, ffn_w_gate_up, ffn_w_down, moe_w_router,
           moe_b_router, moe_w_gate_up, moe_w_down, ln_mix_g, ln_mix_b, ln_ffn_g, ln_ffn_b):
    batch, seq, _ = x_prompt.shape
    n_dec, t_new, _ = x_sample.shape
    past = page_table.shape[1] * PAGE_SIZE
    n_p, n_s = batch * seq, n_dec * t_new
    xp = (x_prompt.reshape(n_p, D_MODEL), 0)
    xs = (x_sample.reshape(n_s, D_MODEL), 0)

    pos = jnp.concatenate([jnp.arange(seq), past + jnp.arange(n_s) % t_new])
    tab_mla = rope_tables(pos, MLA_ROPE // 2)
    tab_dil = rope_tables(pos, DIL_HEAD_DIM // 2)
    tm_p, tm_d = 512, 256
    tm_s = min(tm_p, n_s)
    tm_ds = min(tm_d, n_s)
    prompt_tab = lambda tm: (lambda i: i % (seq // tm))
    sample_tab = lambda tm: (lambda i: seq // tm + i)

    kpe_pool_t = cache_mla_kpe.transpose(0, 1, 3, 2)
    dil_caches = (cache_dil_w128, cache_dil_w512, cache_dil_w2048)
    dil_caches_t = [c.transpose(0, 1, 3, 4, 5, 2).reshape(c.shape[0], c.shape[1], 2 * DIL_WIDTH, c.shape[2])
                    for c in dil_caches]
    row = lambda v: v.reshape(1, -1)
    moe_wgu, moe_wd = moe_w_gate_up.astype(BF16), moe_w_down.astype(BF16)

    ckv_p, kpe_p, ckv_s, kpe_s, kv_p, kv_s = [], [], [], [], [], []
    for i in range(DEPTH):
        li = i // 2
        g_mix, b_mix, g_ffn, b_ffn = row(ln_mix_g[i]), row(ln_mix_b[i]), row(ln_ffn_g[i]), row(ln_ffn_b[i])
        if i % 2 == 0:
            w = _prep_mla(mla_w_in[li], mla_q_norm[li], mla_w_uq[li], mla_kv_norm[li], mla_w_uk[li],
                          mla_w_uv[li], mla_w_o[li])
            c_p, r_p, kfull_p, q_p = mla_project(*xp, n_p, w, tab_mla, prompt_tab(tm_p), BF16, tm_p)
            o_p = mla_prompt_attention(q_p, kfull_p, batch, seq)
            xp = (mla_output(o_p, w, *xp, g_mix, b_mix, tm_p), 0)
            c_s, r_s, _, q_s = mla_project(*xs, n_s, w, tab_mla, sample_tab(tm_s), F32, tm_s)
            o_s = mla_sample_attention(q_s, c_s, r_s, cache_mla_ckv, kpe_pool_t, page_table, li, t_new)
            xs = (mla_output(o_s, w, *xs, g_mix, b_mix, tm_s), 0)
            ckv_p.append(c_p)
            kpe_p.append(r_p[:, :MLA_ROPE])
            ckv_s.append(c_s)
            kpe_s.append(r_s[:, :MLA_ROPE])
            wgu = ffn_w_gate_up[li].astype(BF16)
            wd = ffn_w_down[li].astype(BF16)
            xp = (ffn_norm(xp[0], wgu, wd, g_ffn, b_ffn, 1024, 256), 0)
            xs = (ffn_norm(xs[0], wgu, wd, g_ffn, b_ffn, tm_s, 256), 0)
        else:
            w_in = dil_w_in[li].astype(BF16)
            w_o = dil_w_o[li].astype(BF16)
            h_p, *states = dil_project(*xp, n_p, w_in, tab_dil, prompt_tab(tm_d), tm_d, state_seq=seq)
            o_p = dil_prompt_attention(h_p, batch, seq)
            h_s, = dil_project(*xs, n_s, w_in, tab_dil, sample_tab(tm_ds), tm_ds)
            o_s = dil_sample_attention(h_s.reshape(n_dec, t_new, -1), dil_caches_t, li)
            kv_p.append(states)
            kv_s.append(h_s.reshape(n_dec, t_new, len(DIL_GROUPS), 3, DIL_WIDTH)[:, :, :, 1:])
            xp_rows = lax.slice_in_dim(xp[0], xp[1], xp[1] + n_p) if xp[0].shape[0] != n_p else xp[0]
            xs_rows = lax.slice_in_dim(xs[0], xs[1], xs[1] + n_s) if xs[0].shape[0] != n_s else xs[0]
            x_all, x_chunked = proj_norm(o_p, o_s.reshape(n_s, DIL_WIDTH).astype(BF16), w_o,
                                         xp_rows, xs_rows, g_mix, b_mix, tm_s)
            router = _prep_router(moe_w_router[li], moe_b_router[li])
            y_all = moe_norm(x_all, x_chunked, *router, moe_wgu, moe_wd, li, g_ffn, b_ffn)
            xp, xs = (y_all, 0), (y_all, n_p)

    xp_out = lax.slice_in_dim(xp[0], xp[1], xp[1] + n_p)
    xs_out = lax.slice_in_dim(xs[0], xs[1], xs[1] + n_s)
    outs = [xp_out.reshape(batch, seq, D_MODEL), xs_out.reshape(n_dec, t_new, D_MODEL),
            jnp.stack(ckv_p).reshape(-1, batch, seq, MLA_KV_LORA),
            jnp.stack(kpe_p).reshape(-1, batch, seq, MLA_ROPE),
            jnp.stack(ckv_s).reshape(-1, n_dec, t_new, MLA_KV_LORA),
            jnp.stack(kpe_s).reshape(-1, n_dec, t_new, MLA_ROPE)]
    state_shape = lambda b, rows: (-1, b, rows, 2, DIL_HEADS, DIL_HEAD_DIM)
    for g in range(len(DIL_GROUPS)):
        st = jnp.stack([states[g] for states in kv_p])
        st = st.reshape(st.shape[0], batch, 2, DIL_HEADS, DIL_HEAD_DIM, st.shape[-1])
        outs.append(st.transpose(0, 1, 5, 2, 3, 4))
    for g, cache in enumerate(dil_caches):
        new_rows = jnp.stack([kv[:, :, g] for kv in kv_s]).reshape(state_shape(n_dec, t_new))
        full = jnp.concatenate([cache, new_rows], axis=2)
        keep = min(DIL_GROUPS[g][0], full.shape[2])
        outs.append(full[:, :, full.shape[2] - keep:])
    return tuple(outs)
```
